```python
import jax
import jax.numpy as jnp
from jax import lax
import numpy as np


D_MODEL = 1024
BATCH = 8
SEQ = 4096
DEPTH = 4

N_MIXERS = 2
N_A_LAYERS = (DEPTH + 1) // 2
N_B_LAYERS = DEPTH // 2

HG_EXPAND = 128
HG_HEADS = D_MODEL // HG_EXPAND
HG_KDIM = HG_EXPAND
HG_VDIM = D_MODEL // HG_HEADS
HG_CHUNK = 32

ATT_HEAD_DIM = 128
ATT_HEADS = D_MODEL // ATT_HEAD_DIM
DILATED_GROUPS = ((128, 1), (512, 4), (2048, 16))
N_GROUPS = len(DILATED_GROUPS)
ATT_QKV_COLS = N_GROUPS * 3 * ATT_HEADS * ATT_HEAD_DIM
ROPE_THETA = 500000.0
ROPE_DIM = ATT_HEAD_DIM // 4

D_FF = -(-(8 * D_MODEL) // (3 * 256)) * 256

NORM_EPS = 1e-6

kernel_name = 'hybrid_hgrn2_dilated_attn_block'


def rms_norm(x, g):
    xf = x.astype(jnp.float32)
    y = xf * lax.rsqrt(jnp.mean(xf * xf, axis=-1, keepdims=True) + NORM_EPS)
    return (y * g.astype(jnp.float32)).astype(x.dtype)


def apply_partial_rope(t, cos, sin):
    half = ROPE_DIM // 2
    t1 = t[..., :half]
    t2 = t[..., half:ROPE_DIM]
    return jnp.concatenate([t1 * cos - t2 * sin, t2 * cos + t1 * sin, t[..., ROPE_DIM:]], axis=-1)


def hgrn2_mixer(u, w_in, lb, norm_g, w_out):
    B, S, _ = u.shape
    H, K, C = HG_HEADS, HG_KDIM, HG_CHUNK
    proj = (u @ w_in).astype(jnp.float32).reshape(B, S, 4, H, K)
    q_raw, f_raw, v, g = proj[:, :, 0], proj[:, :, 1], proj[:, :, 2], proj[:, :, 3]
    lb = lb.astype(jnp.float32).reshape(H, K)
    log_f = jnp.log(lb + (1.0 - lb) * jax.nn.sigmoid(f_raw))
    k = (1.0 - lb) * jax.nn.sigmoid(-f_raw)
    q = jax.nn.silu(q_raw)
    NC = S // C
    rs = lambda t: t.reshape(B, NC, C, H, t.shape[-1])
    q, k, v, log_f = rs(q), rs(k), rs(v), rs(log_f)
    b = jnp.cumsum(log_f, axis=2)
    q_dec = q * jnp.exp(b)
    k_inv = k * jnp.exp(-b)
    causal = jnp.tril(jnp.ones((C, C), dtype=bool))
    a = jnp.einsum('bnchk,bnshk->bnhcs', q_dec, k_inv)
    a = jnp.where(causal, a, 0.0)
    o_intra = jnp.einsum('bnhcs,bnshv->bnchv', a, v)
    b_last = b[:, :, -1:]
    k_end = k * jnp.exp(b_last - b)
    chunk_decay = jnp.exp(b_last[:, :, 0])

    def step(state, xs):
        q_c, k_c, v_c, dec = xs
        o_c = jnp.einsum('bchk,bhkv->bchv', q_c, state)
        state = dec[..., None] * state + jnp.einsum('bchk,bchv->bhkv', k_c, v_c)
        return state, o_c

    xs = (jnp.moveaxis(q_dec, 1, 0), jnp.moveaxis(k_end, 1, 0),
          jnp.moveaxis(v, 1, 0), jnp.moveaxis(chunk_decay, 1, 0))
    state0 = jnp.zeros((B, H, K, HG_VDIM), jnp.float32)
    _, o_inter = lax.scan(step, state0, xs)
    o = (o_intra + jnp.moveaxis(o_inter, 0, 1)).reshape(B, S, H, HG_VDIM)
    o = o * lax.rsqrt(jnp.mean(o * o, axis=-1, keepdims=True) + NORM_EPS)
    o = o * norm_g.astype(jnp.float32).reshape(H, HG_VDIM)
    o = o.reshape(B, S, D_MODEL) * jax.nn.silu(g.reshape(B, S, D_MODEL))
    return o.astype(u.dtype) @ w_out


def dilated_group_attention(q, k, v, window, dilation):
    B, S, H, Dh = q.shape
    L = S // dilation
    n_win = window // dilation
    blk = n_win
    nb = -(-L // blk)
    Lp = nb * blk

    def gather(t):
        t = t.reshape(B, L, dilation, H, Dh).transpose(0, 2, 1, 3, 4).reshape(B * dilation, L, H, Dh)
        t = jnp.pad(t, ((0, 0), (0, Lp - L), (0, 0), (0, 0)))
        return t.reshape(B * dilation, nb, blk, H, Dh)

    def with_prev(t):
        prev = jnp.pad(t, ((0, 0), (1, 0), (0, 0), (0, 0), (0, 0)))[:, :-1]
        return jnp.concatenate([prev, t], axis=2)

    qb = gather(q)
    kw = with_prev(gather(k))
    vw = with_prev(gather(v))
    s = jnp.einsum('bnqhd,bnkhd->bnhqk', qb, kw)
    qi = jnp.arange(blk)[:, None]
    ki = jnp.arange(2 * blk)[None, :]
    dist = blk + qi - ki
    band = (dist >= 0) & (dist <= n_win)
    has_prev = (jnp.arange(nb)[:, None, None] > 0) | (ki >= blk)[None]
    valid = band[None] & has_prev
    s = jnp.where(valid[None, :, None], s, -jnp.inf)
    m = jnp.max(s, axis=-1, keepdims=True)
    p = jnp.exp(s - m)
    z = jnp.sum(p, axis=-1, keepdims=True)
    o = jnp.einsum('bnhqk,bnkhd->bnqhd', p, vw) / jnp.swapaxes(z, 2, 3)
    lse = jnp.swapaxes((m + jnp.log(z))[..., 0], 2, 3)

    def scatter_back(t):
        rest = t.shape[3:]
        t = t.reshape(B, dilation, Lp, *rest)[:, :, :L]
        return jnp.moveaxis(t, 1, 2).reshape(B, S, *rest)

    return scatter_back(o), scatter_back(lse)


def dilated_attention_mixer(u, cos, sin, w_in, w_out):
    B, S, _ = u.shape
    proj = (u @ w_in).astype(jnp.float32).reshape(B, S, N_GROUPS, 3, ATT_HEADS, ATT_HEAD_DIM)
    scale = ATT_HEAD_DIM ** -0.5
    outs = []
    lses = []
    for gi, (window, dilation) in enumerate(DILATED_GROUPS):
        q = apply_partial_rope(proj[:, :, gi, 0], cos, sin) * scale
        k = apply_partial_rope(proj[:, :, gi, 1], cos, sin)
        v = proj[:, :, gi, 2]
        o_g, lse_g = dilated_group_attention(q, k, v, window, dilation)
        outs.append(o_g)
        lses.append(lse_g)
    wts = jax.nn.softmax(jnp.stack(lses, axis=0), axis=0)
    o = jnp.einsum('gbsh,gbshd->bshd', wts, jnp.stack(outs, axis=0))
    return o.reshape(B, S, D_MODEL).astype(u.dtype) @ w_out


def swiglu_ffn(u, w_in, w_out):
    gate, up = jnp.split(u @ w_in, 2, axis=-1)
    return (jax.nn.silu(gate) * up) @ w_out


def setup_inputs(seed: int = 0) -> dict:
    key = jax.random.key(seed)
    ks = jax.random.split(key, 16)
    D = D_MODEL
    nrm = jax.random.normal
    x = nrm(ks[0], (BATCH, SEQ, D), jnp.float32)
    c = nrm(ks[1], (BATCH, D), jnp.float32)
    offset = jax.random.randint(ks[2], (BATCH, 1), 0, 1024, dtype=jnp.int32)
    positions = offset + jnp.arange(SEQ, dtype=jnp.int32)[None, :]
    ada_w = nrm(ks[3], (DEPTH, D, 6 * D), jnp.float32) * (0.5 * D ** -0.5)
    ada_b = nrm(ks[4], (DEPTH, 6 * D), jnp.float32) * 0.02
    norm_g = 1.0 + 0.05 * nrm(ks[5], (DEPTH, 4, D), jnp.float32)
    hgrn_w_in = nrm(ks[6], (N_A_LAYERS, D, 4 * D), jnp.float32) * D ** -0.5
    hgrn_lower_bounds = 1.0 + 0.5 * nrm(ks[7], (N_A_LAYERS, D), jnp.float32)
    hgrn_norm_g = 1.0 + 0.05 * nrm(ks[8], (N_A_LAYERS, D), jnp.float32)
    hgrn_w_out = nrm(ks[9], (N_A_LAYERS, D, D), jnp.float32) * D ** -0.5
    attn_w_in = nrm(ks[10], (N_B_LAYERS, D, ATT_QKV_COLS), jnp.float32) * D ** -0.5
    attn_w_out = nrm(ks[11], (N_B_LAYERS, D, D), jnp.float32) * D ** -0.5
    ffn_w_in = nrm(ks[12], (DEPTH, D, 2 * D_FF), jnp.float32) * D ** -0.5
    ffn_w_out = nrm(ks[13], (DEPTH, D_FF, D), jnp.float32) * D_FF ** -0.5
    return {'x': x, 'c': c, 'positions': positions, 'ada_w': ada_w, 'ada_b': ada_b,
            'norm_g': norm_g, 'hgrn_w_in': hgrn_w_in, 'hgrn_lower_bounds': hgrn_lower_bounds,
            'hgrn_norm_g': hgrn_norm_g, 'hgrn_w_out': hgrn_w_out, 'attn_w_in': attn_w_in,
            'attn_w_out': attn_w_out, 'ffn_w_in': ffn_w_in, 'ffn_w_out': ffn_w_out}


def reference(x, c, positions, ada_w, ada_b, norm_g, hgrn_w_in, hgrn_lower_bounds,
              hgrn_norm_g, hgrn_w_out, attn_w_in, attn_w_out, ffn_w_in, ffn_w_out):
    cond = jax.nn.silu(c)
    inv_freq = ROPE_THETA ** (-jnp.arange(0, ROPE_DIM, 2, dtype=jnp.float32) / ROPE_DIM)
    ang = positions.astype(jnp.float32)[..., None] * inv_freq
    cos = jnp.cos(ang)[:, :, None, :]
    sin = jnp.sin(ang)[:, :, None, :]
    lbs = jnp.cumsum(jax.nn.softmax(hgrn_lower_bounds.astype(jnp.float32), axis=0), axis=0)
    lbs = lbs - lbs[0:1]
    h = x
    for layer in range(DEPTH):
        mod = (cond @ ada_w[layer] + ada_b[layer])[:, None, :]
        sh1, sc1, g1, sh2, sc2, g2 = jnp.split(mod, 6, axis=-1)
        u = rms_norm(h, norm_g[layer, 0]) * (1.0 + sc1) + sh1
        idx = layer // N_MIXERS
        if layer % N_MIXERS == 0:
            y = hgrn2_mixer(u, hgrn_w_in[idx], lbs[idx], hgrn_norm_g[idx], hgrn_w_out[idx])
        else:
            y = dilated_attention_mixer(u, cos, sin, attn_w_in[idx], attn_w_out[idx])
        h = h + (1.0 + g1) * rms_norm(y, norm_g[layer, 1])
        u = rms_norm(h, norm_g[layer, 2]) * (1.0 + sc2) + sh2
        y = swiglu_ffn(u, ffn_w_in[layer], ffn_w_out[layer])
        h = h + (1.0 + g2) * rms_norm(y, norm_g[layer, 3])
    return h
```

```python
import functools

import jax
import jax.numpy as jnp
from jax import lax
from jax.experimental import pallas as pl
from jax.experimental.pallas import tpu as pltpu

F32 = jnp.float32
BF16 = jnp.bfloat16

D_MODEL = 1024
HEAD_DIM = 128
N_HEADS = D_MODEL // HEAD_DIM
HG_CHUNK = 32
HG_BLOCK = 128
ATT_BLOCK = 128
DILATIONS = (1, 4, 16)
ROPE_DIM = HEAD_DIM // 4
ROPE_HALF = ROPE_DIM // 2
ROPE_THETA = 500000.0
D_FF = -(-(8 * D_MODEL) // (3 * 256)) * 256
NORM_EPS = 1e-6
VMEM_LIMIT_V7X = 56 * 1024 * 1024


def _params(sem, vmem=None):
    return pltpu.CompilerParams(dimension_semantics=sem, vmem_limit_bytes=vmem)


def _rms(x):
    return x * lax.rsqrt(jnp.mean(x * x, axis=-1, keepdims=True) + NORM_EPS)


def _silu(x):
    return x * jax.nn.sigmoid(x)


def _mod_kernel(c_ref, w_ref, b_ref, o_ref):
    cond = _silu(c_ref[...])
    o_ref[...] = jnp.dot(cond, w_ref[...], preferred_element_type=F32,
                         precision=lax.Precision.HIGHEST) + b_ref[...]


def _modulation(c, ada_w, ada_b):
    depth, d, n = ada_w.shape
    bsz = c.shape[0]
    tn = 1536
    return pl.pallas_call(
        _mod_kernel,
        out_shape=jax.ShapeDtypeStruct((depth, bsz, n), F32),
        grid=(depth, n // tn),
        in_specs=[
            pl.BlockSpec((bsz, d), lambda l, j: (0, 0)),
            pl.BlockSpec((None, d, tn), lambda l, j: (l, 0, j)),
            pl.BlockSpec((None, 1, tn), lambda l, j: (l, 0, j)),
        ],
        out_specs=pl.BlockSpec((None, bsz, tn), lambda l, j: (l, 0, j)),
        compiler_params=_params(("parallel", "parallel")),
        name="adaln_modulation",
    )(c, ada_w, ada_b.reshape(depth, 1, n))


def _rope_kernel(pos_ref, invf_ref, sgn_ref, cos_ref, sin_ref):
    ang = pos_ref[...].astype(F32) * invf_ref[...]
    cos_ref[...] = jnp.cos(ang)
    sin_ref[...] = jnp.sin(ang) * sgn_ref[...]


def _rope_tables(positions):
    bsz, seq = positions.shape
    ts = 1024
    inv_freq = ROPE_THETA ** (-jnp.arange(0, ROPE_DIM, 2, dtype=F32) / ROPE_DIM)
    zeros = jnp.zeros((HEAD_DIM // 2 - ROPE_HALF,), F32)
    invf = jnp.concatenate([inv_freq, zeros, inv_freq, zeros]).reshape(1, HEAD_DIM)
    sgn = jnp.concatenate([-jnp.ones((ROPE_HALF,), F32), zeros,
                           jnp.ones((ROPE_HALF,), F32), zeros]).reshape(1, HEAD_DIM)
    out = jax.ShapeDtypeStruct((bsz, seq, HEAD_DIM), F32)
    return pl.pallas_call(
        _rope_kernel,
        out_shape=(out, out),
        grid=(bsz, seq // ts),
        in_specs=[
            pl.BlockSpec((None, ts, 1), lambda b, i: (b, i, 0)),
            pl.BlockSpec((1, HEAD_DIM), lambda b, i: (0, 0)),
            pl.BlockSpec((1, HEAD_DIM), lambda b, i: (0, 0)),
        ],
        out_specs=(pl.BlockSpec((None, ts, HEAD_DIM), lambda b, i: (b, i, 0)),
                   pl.BlockSpec((None, ts, HEAD_DIM), lambda b, i: (b, i, 0))),
        compiler_params=_params(("parallel", "parallel")),
        name="rope_tables",
    )(positions.reshape(bsz, seq, 1), invf, sgn)


def _rope_layout(w):
    d = w.shape[0]
    w = w.reshape(d, len(DILATIONS), 3, N_HEADS, HEAD_DIM)
    qk = w[:, :, :2]
    qk = jnp.concatenate([qk[..., :ROPE_HALF], qk[..., ROPE_DIM:HEAD_DIM // 2 + ROPE_HALF],
                          qk[..., ROPE_HALF:ROPE_DIM], qk[..., HEAD_DIM // 2 + ROPE_HALF:]], axis=-1)
    return jnp.concatenate([qk, w[:, :, 2:]], axis=2).reshape(d, -1)


def _normed_input(h_ref, mod_ref, g_ref, shift_row):
    y = _rms(h_ref[...]) * g_ref[...]
    return y * (1.0 + mod_ref[shift_row + 1:shift_row + 2, :]) + mod_ref[shift_row:shift_row + 1, :]


def _inproj_kernel(h_ref, mod_ref, g_ref, w_ref, o_ref, u_scr):
    @pl.when(pl.program_id(2) == 0)
    def _():
        u_scr[...] = _normed_input(h_ref, mod_ref, g_ref, 0).astype(BF16)

    o_ref[...] = jnp.dot(u_scr[...], w_ref[...], preferred_element_type=F32).astype(o_ref.dtype)


def _inproj_rope_kernel(h_ref, mod_ref, g_ref, cos_ref, sin_ref, w_ref, o_ref, u_scr):
    j = pl.program_id(2)

    @pl.when(j == 0)
    def _():
        u_scr[...] = _normed_input(h_ref, mod_ref, g_ref, 0).astype(BF16)

    acc = jnp.dot(u_scr[...], w_ref[...], preferred_element_type=F32)
    kind = j % 3

    @pl.when(kind == 2)
    def _():
        o_ref[...] = acc.astype(o_ref.dtype)

    @pl.when(kind != 2)
    def _():
        scale = jnp.where(kind == 0, HEAD_DIM ** -0.5, 1.0).astype(F32)
        cos = cos_ref[...] * scale
        sin = sin_ref[...] * scale
        for hd in range(N_HEADS):
            sl = slice(hd * HEAD_DIM, (hd + 1) * HEAD_DIM)
            t = acc[:, sl]
            partner = pltpu.roll(t, HEAD_DIM // 2, axis=1)
            o_ref[:, sl] = (t * cos + partner * sin).astype(o_ref.dtype)


def _in_projection(h, mod_l, gain, w, out_dtype, rope=None, tm=1024, tn=1024):
    bsz, seq, d = h.shape
    n = w.shape[1]
    in_specs = [
        pl.BlockSpec((None, tm, d), lambda b, i, j: (b, i, 0)),
        pl.BlockSpec((None, 6, d), lambda b, i, j: (b, 0, 0)),
        pl.BlockSpec((1, d), lambda b, i, j: (0, 0)),
    ]
    args = [h, mod_l, gain]
    if rope is None:
        body = _inproj_kernel
    else:
        body = _inproj_rope_kernel
        in_specs += [pl.BlockSpec((None, tm, HEAD_DIM), lambda b, i, j: (b, i, 0))] * 2
        args += list(rope)
    in_specs.append(pl.BlockSpec((d, tn), lambda b, i, j: (0, j)))
    args.append(w)
    return pl.pallas_call(
        body,
        out_shape=jax.ShapeDtypeStruct((bsz, seq, n), out_dtype),
        grid=(bsz, seq // tm, n // tn),
        in_specs=in_specs,
        out_specs=pl.BlockSpec((None, tm, tn), lambda b, i, j: (b, i, j)),
        scratch_shapes=[pltpu.VMEM((tm, d), BF16)],
        compiler_params=_params(("parallel", "parallel", "arbitrary"), VMEM_LIMIT_V7X),
        name="in_projection",
    )(*args)


def _chunk_rows(vals, width):
    return jnp.concatenate([jnp.broadcast_to(v, (HG_CHUNK, width)) for v in vals], axis=0)


def _hgrn_kernel(p_ref, lb_ref, ng_ref, o_ref, state, *, layer_idx):
    d = D_MODEL

    @pl.when(pl.program_id(1) == 0)
    def _():
        state[...] = jnp.zeros_like(state)

    lbr = lb_ref[...]
    e = jnp.exp(lbr - jnp.max(lbr, axis=0, keepdims=True))
    sm = e / jnp.sum(e, axis=0, keepdims=True)
    csum = sm[0:1, :]
    first = csum
    for r in range(1, layer_idx + 1):
        csum = csum + sm[r:r + 1, :]
    lb = csum - first

    q_raw = p_ref[:, 0:d]
    f_raw = p_ref[:, d:2 * d]
    log_f = jnp.log(lb + (1.0 - lb) * jax.nn.sigmoid(f_raw))
    k = (1.0 - lb) * jax.nn.sigmoid(-f_raw)
    q = _silu(q_raw)

    ti = lax.broadcasted_iota(jnp.int32, (HG_BLOCK, HG_BLOCK), 0)
    si = lax.broadcasted_iota(jnp.int32, (HG_BLOCK, HG_BLOCK), 1)
    same_chunk = (ti // HG_CHUNK) == (si // HG_CHUNK)
    diag_mask = same_chunk & (si <= ti)
    off_mask = (si // HG_CHUNK) < (ti // HG_CHUNK)
    tril = jnp.where(diag_mask, 1.0, 0.0).astype(BF16)
    hi = log_f.astype(BF16)
    rem = log_f - hi.astype(F32)
    mid = rem.astype(BF16)
    lo = (rem - mid.astype(F32)).astype(BF16)
    b = (jnp.dot(tril, hi, preferred_element_type=F32)
         + jnp.dot(tril, mid, preferred_element_type=F32)
         + jnp.dot(tril, lo, preferred_element_type=F32))

    nchunk = HG_BLOCK // HG_CHUNK
    bl = [b[(j + 1) * HG_CHUNK - 1:(j + 1) * HG_CHUNK, :] for j in range(nchunk)]
    zero = jnp.zeros_like(bl[0])
    b_tot = bl[0] + bl[1] + bl[2] + bl[3]
    q_dec = q * jnp.exp(b)
    k_inv = k * jnp.exp(-b)
    q_mid = q_dec * _chunk_rows([jnp.exp(zero), jnp.exp(-bl[1]), jnp.exp(zero), jnp.exp(bl[2])], d)
    k_mid = k_inv * _chunk_rows([jnp.exp(bl[0] + bl[1]), jnp.exp(bl[1]), jnp.exp(zero), jnp.exp(zero)], d)
    q_blk = q_dec * _chunk_rows([jnp.exp(zero), jnp.exp(bl[0]), jnp.exp(bl[0] + bl[1]),
                                 jnp.exp(bl[0] + bl[1] + bl[2])], d)
    k_blk = k_inv * _chunk_rows([jnp.exp(b_tot), jnp.exp(b_tot - bl[0]), jnp.exp(bl[2] + bl[3]),
                                 jnp.exp(bl[3])], d)
    dec = jnp.exp(b_tot)

    nt = (((1,), (1,)), ((), ()))
    tn = (((0,), (0,)), ((), ()))
    for hd in range(N_HEADS):
        sl = slice(hd * HEAD_DIM, (hd + 1) * HEAD_DIM)
        v = p_ref[:, 2 * d + hd * HEAD_DIM:2 * d + (hd + 1) * HEAD_DIM].astype(BF16)
        gate = p_ref[:, 3 * d + hd * HEAD_DIM:3 * d + (hd + 1) * HEAD_DIM]
        a_diag = lax.dot_general(q_dec[:, sl].astype(BF16), k_inv[:, sl].astype(BF16), nt,
                                 preferred_element_type=F32)
        a_off = lax.dot_general(q_mid[:, sl].astype(BF16), k_mid[:, sl].astype(BF16), nt,
                                preferred_element_type=F32)
        a = jnp.where(diag_mask, a_diag, jnp.where(off_mask, a_off, 0.0))
        s0 = state[hd]
        o = (jnp.dot(a.astype(BF16), v, preferred_element_type=F32)
             + jnp.dot(q_blk[:, sl].astype(BF16), s0.astype(BF16), preferred_element_type=F32))
        dec_col = jnp.transpose(jnp.broadcast_to(dec[:, sl], (HEAD_DIM, HEAD_DIM)))
        state[hd] = dec_col * s0 + lax.dot_general(k_blk[:, sl].astype(BF16), v, tn,
                                                   preferred_element_type=F32)
        o = _rms(o) * ng_ref[:, sl]
        o_ref[:, sl] = (o * _silu(gate)).astype(o_ref.dtype)


def _hgrn_mix(proj, lower_bounds, norm_g, layer_idx):
    bsz, seq, n = proj.shape
    d = D_MODEL
    n_layers = lower_bounds.shape[0]
    return pl.pallas_call(
        functools.partial(_hgrn_kernel, layer_idx=layer_idx),
        out_shape=jax.ShapeDtypeStruct((bsz, seq, d), BF16),
        grid=(bsz, seq // HG_BLOCK),
        in_specs=[
            pl.BlockSpec((None, HG_BLOCK, n), lambda b, i: (b, i, 0)),
            pl.BlockSpec((n_layers, d), lambda b, i: (0, 0)),
            pl.BlockSpec((1, d), lambda b, i: (0, 0)),
        ],
        out_specs=pl.BlockSpec((None, HG_BLOCK, d), lambda b, i: (b, i, 0)),
        scratch_shapes=[pltpu.VMEM((N_HEADS, HEAD_DIM, HEAD_DIM), F32)],
        compiler_params=_params(("parallel", "arbitrary"), VMEM_LIMIT_V7X),
        name="hgrn2_recurrence",
    )(proj, lower_bounds, norm_g)


def _attn_kernel(q_ref, kp_ref, kc_ref, vp_ref, vc_ref, o_ref, lse_ref, *, tq):
    first_tile = pl.program_id(2) == 0
    qi = lax.broadcasted_iota(jnp.int32, (ATT_BLOCK, 2 * ATT_BLOCK), 0)
    ki = lax.broadcasted_iota(jnp.int32, (ATT_BLOCK, 2 * ATT_BLOCK), 1)
    band = (ki >= qi) & (ki <= qi + ATT_BLOCK)
    lane = lax.broadcasted_iota(jnp.int32, (ATT_BLOCK, HEAD_DIM), 1)
    nt = (((1,), (1,)), ((), ()))
    for n in range(tq // ATT_BLOCK):
        rows = slice(n * ATT_BLOCK, (n + 1) * ATT_BLOCK)
        if n == 0:
            k_prev, v_prev = kp_ref, vp_ref
            prev_rows = slice(0, ATT_BLOCK)
            valid = band & (jnp.logical_not(first_tile) | (ki >= ATT_BLOCK))
        else:
            k_prev, v_prev = kc_ref, vc_ref
            prev_rows = slice((n - 1) * ATT_BLOCK, n * ATT_BLOCK)
            valid = band
        lse_tile = jnp.zeros((ATT_BLOCK, HEAD_DIM), F32)
        for hd in range(N_HEADS):
            sl = slice(hd * HEAD_DIM, (hd + 1) * HEAD_DIM)
            kw = jnp.concatenate([k_prev[prev_rows, sl], kc_ref[rows, sl]], axis=0)
            vw = jnp.concatenate([v_prev[prev_rows, sl], vc_ref[rows, sl]], axis=0)
            s = lax.dot_general(q_ref[rows, sl], kw, nt, preferred_element_type=F32)
            s = jnp.where(valid, s, -jnp.inf)
            m = jnp.max(s, axis=-1, keepdims=True)
            p = jnp.exp(s - m)
            z = jnp.sum(p, axis=-1, keepdims=True)
            o = jnp.dot(p.astype(BF16), vw, preferred_element_type=F32) / z
            o_ref[rows, sl] = o
            lse_tile = jnp.where(lane == hd, m + jnp.log(z), lse_tile)
        lse_ref[rows, :] = lse_tile


def _dilated_attention(qkv, group, dilation):
    bsz, seq, n = qkv.shape
    d = D_MODEL
    sub = seq // dilation
    tq = min(sub, 512)
    blk_per_tile = tq // ATT_BLOCK
    ncol = n // d
    view = qkv.reshape(bsz, sub, dilation * n)
    base = 3 * group

    def col(which):
        return lambda b, r, i: (b, i, r * ncol + base + which)

    def col_prev(which):
        return lambda b, r, i: (b, jnp.maximum(i * blk_per_tile - 1, 0), r * ncol + base + which)

    o, lse = pl.pallas_call(
        functools.partial(_attn_kernel, tq=tq),
        out_shape=(jax.ShapeDtypeStruct((bsz, sub, dilation * d), F32),
                   jax.ShapeDtypeStruct((bsz, sub, dilation * HEAD_DIM), F32)),
        grid=(bsz, dilation, sub // tq),
        in_specs=[
            pl.BlockSpec((None, tq, d), col(0)),
            pl.BlockSpec((None, ATT_BLOCK, d), col_prev(1)),
            pl.BlockSpec((None, tq, d), col(1)),
            pl.BlockSpec((None, ATT_BLOCK, d), col_prev(2)),
            pl.BlockSpec((None, tq, d), col(2)),
        ],
        out_specs=(pl.BlockSpec((None, tq, d), lambda b, r, i: (b, i, r)),
                   pl.BlockSpec((None, tq, HEAD_DIM), lambda b, r, i: (b, i, r))),
        compiler_params=_params(("parallel", "parallel", "parallel"), VMEM_LIMIT_V7X),
        name="dilated_attention",
    )(view, view, view, view, view)
    return o.reshape(bsz, seq, d), lse.reshape(bsz, seq, HEAD_DIM)


def _residual_update(y, h_ref, mod_ref, g_ref, gate_row, o_ref):
    o_ref[...] = h_ref[...] + (1.0 + mod_ref[gate_row:gate_row + 1, :]) * (_rms(y) * g_ref[...])


def _outproj_kernel(x_ref, w_ref, h_ref, mod_ref, g_ref, o_ref):
    y = jnp.dot(x_ref[...], w_ref[...], preferred_element_type=F32)
    _residual_update(y, h_ref, mod_ref, g_ref, 2, o_ref)


def _attn_outproj_kernel(o1_ref, o2_ref, o3_ref, l1_ref, l2_ref, l3_ref, w_ref, h_ref, mod_ref,
                         g_ref, o_ref):
    l1, l2, l3 = l1_ref[...], l2_ref[...], l3_ref[...]
    m = jnp.maximum(jnp.maximum(l1, l2), l3)
    e1, e2, e3 = jnp.exp(l1 - m), jnp.exp(l2 - m), jnp.exp(l3 - m)
    z = e1 + e2 + e3
    w1, w2, w3 = e1 / z, e2 / z, e3 / z
    pieces = []
    for hd in range(N_HEADS):
        sl = slice(hd * HEAD_DIM, (hd + 1) * HEAD_DIM)
        o = (w1[:, hd:hd + 1] * o1_ref[:, sl] + w2[:, hd:hd + 1] * o2_ref[:, sl]
             + w3[:, hd:hd + 1] * o3_ref[:, sl])
        pieces.append(o.astype(BF16))
    x = jnp.concatenate(pieces, axis=1)
    y = jnp.dot(x, w_ref[...], preferred_element_type=F32)
    _residual_update(y, h_ref, mod_ref, g_ref, 2, o_ref)


def _out_projection(xs, lses, w, h, mod_l, gain, tm=512):
    bsz, seq, d = h.shape
    tok = lambda b, i: (b, i, 0)
    in_specs = [pl.BlockSpec((None, tm, x.shape[-1]), tok) for x in xs]
    in_specs += [pl.BlockSpec((None, tm, HEAD_DIM), tok) for _ in lses]
    in_specs += [
        pl.BlockSpec(w.shape, lambda b, i: (0, 0)),
        pl.BlockSpec((None, tm, d), tok),
        pl.BlockSpec((None, 6, d), lambda b, i: (b, 0, 0)),
        pl.BlockSpec((1, d), lambda b, i: (0, 0)),
    ]
    return pl.pallas_call(
        _attn_outproj_kernel if lses else _outproj_kernel,
        out_shape=jax.ShapeDtypeStruct(h.shape, F32),
        grid=(bsz, seq // tm),
        in_specs=in_specs,
        out_specs=pl.BlockSpec((None, tm, d), tok),
        compiler_params=_params(("parallel", "parallel"), VMEM_LIMIT_V7X),
        name="out_projection",
    )(*xs, *lses, w, h, mod_l, gain)


FFN_CHUNK = 256


def _ffn_kernel(h_ref, mod_ref, g_in_ref, g_out_ref, w_in_ref, w_out_ref, o_ref):
    u = _normed_input(h_ref, mod_ref, g_in_ref, 3).astype(BF16)
    acc = jnp.zeros(h_ref.shape, F32)
    for c0 in range(0, D_FF, FFN_CHUNK):
        gate = jnp.dot(u, w_in_ref[:, c0:c0 + FFN_CHUNK], preferred_element_type=F32)
        up = jnp.dot(u, w_in_ref[:, D_FF + c0:D_FF + c0 + FFN_CHUNK], preferred_element_type=F32)
        act = (_silu(gate) * up).astype(BF16)
        acc = acc + jnp.dot(act, w_out_ref[c0:c0 + FFN_CHUNK, :], preferred_element_type=F32)
    _residual_update(acc, h_ref, mod_ref, g_out_ref, 5, o_ref)


def _ffn(h, mod_l, g_in, g_out, w_in, w_out, tm=512):
    bsz, seq, d = h.shape
    tok = lambda b, i: (b, i, 0)
    const = lambda b, i: (0, 0)
    return pl.pallas_call(
        _ffn_kernel,
        out_shape=jax.ShapeDtypeStruct(h.shape, F32),
        grid=(bsz, seq // tm),
        in_specs=[
            pl.BlockSpec((None, tm, d), tok),
            pl.BlockSpec((None, 6, d), lambda b, i: (b, 0, 0)),
            pl.BlockSpec((1, d), const),
            pl.BlockSpec((1, d), const),
            pl.BlockSpec(w_in.shape, const, pipeline_mode=pl.Buffered(1)),
            pl.BlockSpec(w_out.shape, const, pipeline_mode=pl.Buffered(1)),
        ],
        out_specs=pl.BlockSpec((None, tm, d), tok),
        compiler_params=_params(("parallel", "parallel"), VMEM_LIMIT_V7X),
        name="swiglu_ffn",
    )(h, mod_l, g_in, g_out, w_in, w_out)


def kernel(x, c, positions, ada_w, ada_b, norm_g, hgrn_w_in, hgrn_lower_bounds, hgrn_norm_g,
           hgrn_w_out, attn_w_in, attn_w_out, ffn_w_in, ffn_w_out):
    depth = ada_w.shape[0]
    bsz, seq, d = x.shape
    mod = _modulation(c, ada_w, ada_b).reshape(depth, bsz, 6, d)
    rope = _rope_tables(positions)
    h = x
    for layer in range(depth):
        mod_l = mod[layer]
        gains = norm_g[layer].reshape(4, 1, d)
        idx = layer // 2
        if layer % 2 == 0:
            proj = _in_projection(h, mod_l, gains[0], hgrn_w_in[idx].astype(BF16), F32)
            mixed = _hgrn_mix(proj, hgrn_lower_bounds, hgrn_norm_g[idx].reshape(1, d), idx)
            h = _out_projection([mixed], [], hgrn_w_out[idx].astype(BF16), h, mod_l, gains[1])
        else:
            w_in = _rope_layout(attn_w_in[idx]).astype(BF16)
            qkv = _in_projection(h, mod_l, gains[0], w_in, BF16, rope=rope)
            outs = [_dilated_attention(qkv, g, dil) for g, dil in enumerate(DILATIONS)]
            h = _out_projection([o for o, _ in outs], [l for _, l in outs],
                                attn_w_out[idx].astype(BF16), h, mod_l, gains[1])
        h = _ffn(h, mod_l, gains[2], gains[3], ffn_w_in[layer].astype(BF16),
                 ffn_w_out[layer].astype(BF16))
    return h
```

```python
import functools

import jax
import jax.numpy as jnp
from jax import lax
from jax.experimental import pallas as pl
from jax.experimental.pallas import tpu as pltpu

F32 = jnp.float32
BF16 = jnp.bfloat16

D_MODEL = 1024
HEAD_DIM = 128
N_HEADS = D_MODEL // HEAD_DIM
HG_CHUNK = 32
HG_BLOCK = 128
ATT_BLOCK = 128
DILATIONS = (1, 4, 16)
ROPE_DIM = HEAD_DIM // 4
ROPE_HALF = ROPE_DIM // 2
ROPE_THETA = 500000.0
D_FF = -(-(8 * D_MODEL) // (3 * 256)) * 256
NORM_EPS = 1e-6
VMEM_LIMIT_V7X = 56 * 1024 * 1024


def _params(sem, vmem=None):
    return pltpu.CompilerParams(dimension_semantics=sem, vmem_limit_bytes=vmem)


def _rms(x):
    return x * lax.rsqrt(jnp.mean(x * x, axis=-1, keepdims=True) + NORM_EPS)


def _silu(x):
    return x * jax.nn.sigmoid(x)


def _mod_kernel(c_ref, w_ref, b_ref, o_ref):
    cond = _silu(c_ref[...])
    o_ref[...] = jnp.dot(cond, w_ref[...], preferred_element_type=F32,
                         precision=lax.Precision.HIGHEST) + b_ref[...]


def _modulation(c, ada_w, ada_b):
    depth, d, n = ada_w.shape
    bsz = c.shape[0]
    tn = 1536
    return pl.pallas_call(
        _mod_kernel,
        out_shape=jax.ShapeDtypeStruct((depth, bsz, n), F32),
        grid=(depth, n // tn),
        in_specs=[
            pl.BlockSpec((bsz, d), lambda l, j: (0, 0)),
            pl.BlockSpec((None, d, tn), lambda l, j: (l, 0, j)),
            pl.BlockSpec((None, 1, tn), lambda l, j: (l, 0, j)),
        ],
        out_specs=pl.BlockSpec((None, bsz, tn), lambda l, j: (l, 0, j)),
        compiler_params=_params(("parallel", "parallel")),
        name="adaln_modulation",
    )(c, ada_w, ada_b.reshape(depth, 1, n))


def _rope_kernel(pos_ref, invf_ref, sgn_ref, cos_ref, sin_ref):
    ang = pos_ref[...].astype(F32) * invf_ref[...]
    cos_ref[...] = jnp.cos(ang)
    sin_ref[...] = jnp.sin(ang) * sgn_ref[...]


def _rope_tables(positions):
    bsz, seq = positions.shape
    ts = 1024
    inv_freq = ROPE_THETA ** (-jnp.arange(0, ROPE_DIM, 2, dtype=F32) / ROPE_DIM)
    zeros = jnp.zeros((HEAD_DIM // 2 - ROPE_HALF,), F32)
    invf = jnp.concatenate([inv_freq, zeros, inv_freq, zeros]).reshape(1, HEAD_DIM)
    sgn = jnp.concatenate([-jnp.ones((ROPE_HALF,), F32), zeros,
                           jnp.ones((ROPE_HALF,), F32), zeros]).reshape(1, HEAD_DIM)
    out = jax.ShapeDtypeStruct((bsz, seq, HEAD_DIM), F32)
    return pl.pallas_call(
        _rope_kernel,
        out_shape=(out, out),
        grid=(bsz, seq // ts),
        in_specs=[
            pl.BlockSpec((None, ts, 1), lambda b, i: (b, i, 0)),
            pl.BlockSpec((1, HEAD_DIM), lambda b, i: (0, 0)),
            pl.BlockSpec((1, HEAD_DIM), lambda b, i: (0, 0)),
        ],
        out_specs=(pl.BlockSpec((None, ts, HEAD_DIM), lambda b, i: (b, i, 0)),
                   pl.BlockSpec((None, ts, HEAD_DIM), lambda b, i: (b, i, 0))),
        compiler_params=_params(("parallel", "parallel")),
        name="rope_tables",
    )(positions.reshape(bsz, seq, 1), invf, sgn)


def _rope_layout(w):
    d = w.shape[0]
    w = w.reshape(d, len(DILATIONS), 3, N_HEADS, HEAD_DIM)
    qk = w[:, :, :2]
    qk = jnp.concatenate([qk[..., :ROPE_HALF], qk[..., ROPE_DIM:HEAD_DIM // 2 + ROPE_HALF],
                          qk[..., ROPE_HALF:ROPE_DIM], qk[..., HEAD_DIM // 2 + ROPE_HALF:]], axis=-1)
    return jnp.concatenate([qk, w[:, :, 2:]], axis=2).reshape(d, -1)


def _normed_input(h_ref, mod_ref, g_ref, shift_row):
    y = _rms(h_ref[...]) * g_ref[...]
    return y * (1.0 + mod_ref[shift_row + 1:shift_row + 2, :]) + mod_ref[shift_row:shift_row + 1, :]


def _inproj_kernel(h_ref, mod_ref, g_ref, w_ref, o_ref, u_scr):
    @pl.when(pl.program_id(2) == 0)
    def _():
        u_scr[...] = _normed_input(h_ref, mod_ref, g_ref, 0).astype(BF16)

    o_ref[...] = jnp.dot(u_scr[...], w_ref[...], preferred_element_type=F32).astype(o_ref.dtype)


def _residue_rows(ref, dilation):
    if dilation == 1:
        return ref[...]
    n = ref.shape[0] // dilation
    return jnp.concatenate([ref[pl.ds(r, n, stride=dilation), :] for r in range(dilation)], axis=0)


def _inproj_rope_kernel(*refs, dilation):
    h_slabs = refs[:-7]
    mod_ref, g_ref, cos_ref, sin_ref, w_ref, o_ref, u_scr = refs[-7:]
    kind = pl.program_id(2)
    n = u_scr.shape[0] // dilation

    @pl.when(kind == 0)
    def _():
        x = jnp.concatenate([_residue_rows(ref, dilation) for ref in h_slabs], axis=1)
        y = _rms(x) * g_ref[...]
        u_scr[...] = (y * (1.0 + mod_ref[1:2, :]) + mod_ref[0:1, :]).astype(BF16)

    acc = jnp.dot(u_scr[...], w_ref[...], preferred_element_type=F32)

    @pl.when(kind == 2)
    def _():
        for r in range(dilation):
            o_ref[r] = acc[r * n:(r + 1) * n, :].astype(o_ref.dtype)

    @pl.when(kind != 2)
    def _():
        scale = jnp.where(kind == 0, HEAD_DIM ** -0.5, 1.0).astype(F32)
        cos = _residue_rows(cos_ref, dilation) * scale
        sin = _residue_rows(sin_ref, dilation) * scale
        for hd in range(N_HEADS):
            sl = slice(hd * HEAD_DIM, (hd + 1) * HEAD_DIM)
            t = acc[:, sl]
            partner = pltpu.roll(t, HEAD_DIM // 2, axis=1)
            val = (t * cos + partner * sin).astype(o_ref.dtype)
            for r in range(dilation):
                o_ref[r, :, sl] = val[r * n:(r + 1) * n, :]


def _in_projection(h, mod_l, gain, w, tm=1024, tn=1024):
    bsz, seq, d = h.shape
    n = w.shape[1]
    return pl.pallas_call(
        _inproj_kernel,
        out_shape=jax.ShapeDtypeStruct((bsz, seq, n), F32),
        grid=(bsz, seq // tm, n // tn),
        in_specs=[
            pl.BlockSpec((None, tm, d), lambda b, i, j: (b, i, 0)),
            pl.BlockSpec((None, 6, d), lambda b, i, j: (b, 0, 0)),
            pl.BlockSpec((1, d), lambda b, i, j: (0, 0)),
            pl.BlockSpec((d, tn), lambda b, i, j: (0, j)),
        ],
        out_specs=pl.BlockSpec((None, tm, tn), lambda b, i, j: (b, i, j)),
        scratch_shapes=[pltpu.VMEM((tm, d), BF16)],
        compiler_params=_params(("parallel", "parallel", "arbitrary"), VMEM_LIMIT_V7X),
        name="in_projection",
    )(h, mod_l, gain, w)


def _attn_in_projection(h, mod_l, gain, rope, w, group, dilation, tm=1024):
    bsz, seq, d = h.shape
    tok = lambda b, i, j: (b, i, 0)
    if dilation == 1:
        h_specs = [pl.BlockSpec((None, tm, d), tok)]
    else:
        h_specs = [pl.BlockSpec((None, tm, HEAD_DIM), functools.partial(lambda b, i, j, c: (b, i, c), c=c))
                   for c in range(d // HEAD_DIM)]
    return pl.pallas_call(
        functools.partial(_inproj_rope_kernel, dilation=dilation),
        out_shape=jax.ShapeDtypeStruct((bsz, dilation, seq // dilation, 3 * d), BF16),
        grid=(bsz, seq // tm, 3),
        in_specs=h_specs + [
            pl.BlockSpec((None, 6, d), lambda b, i, j: (b, 0, 0)),
            pl.BlockSpec((1, d), lambda b, i, j: (0, 0)),
            pl.BlockSpec((None, tm, HEAD_DIM), tok),
            pl.BlockSpec((None, tm, HEAD_DIM), tok),
            pl.BlockSpec((d, d), lambda b, i, j: (0, 3 * group + j)),
        ],
        out_specs=pl.BlockSpec((None, dilation, tm // dilation, d), lambda b, i, j: (b, 0, i, j)),
        scratch_shapes=[pltpu.VMEM((tm, d), BF16)],
        compiler_params=_params(("parallel", "parallel", "arbitrary"), VMEM_LIMIT_V7X),
        name="attn_in_projection",
    )(*[h] * len(h_specs), mod_l, gain, rope[0], rope[1], w)


def _chunk_rows(vals, width):
    return jnp.concatenate([jnp.broadcast_to(v, (HG_CHUNK, width)) for v in vals], axis=0)


def _hgrn_kernel(p_ref, lb_ref, ng_ref, o_ref, state, *, layer_idx):
    d = D_MODEL

    @pl.when(pl.program_id(1) == 0)
    def _():
        state[...] = jnp.zeros_like(state)

    lbr = lb_ref[...]
    e = jnp.exp(lbr - jnp.max(lbr, axis=0, keepdims=True))
    sm = e / jnp.sum(e, axis=0, keepdims=True)
    csum = sm[0:1, :]
    first = csum
    for r in range(1, layer_idx + 1):
        csum = csum + sm[r:r + 1, :]
    lb = csum - first

    q_raw = p_ref[:, 0:d]
    f_raw = p_ref[:, d:2 * d]
    log_f = jnp.log(lb + (1.0 - lb) * jax.nn.sigmoid(f_raw))
    k = (1.0 - lb) * jax.nn.sigmoid(-f_raw)
    q = _silu(q_raw)

    ti = lax.broadcasted_iota(jnp.int32, (HG_BLOCK, HG_BLOCK), 0)
    si = lax.broadcasted_iota(jnp.int32, (HG_BLOCK, HG_BLOCK), 1)
    same_chunk = (ti // HG_CHUNK) == (si // HG_CHUNK)
    diag_mask = same_chunk & (si <= ti)
    off_mask = (si // HG_CHUNK) < (ti // HG_CHUNK)
    tril = jnp.where(diag_mask, 1.0, 0.0).astype(BF16)
    hi = log_f.astype(BF16)
    rem = log_f - hi.astype(F32)
    mid = rem.astype(BF16)
    lo = (rem - mid.astype(F32)).astype(BF16)
    b = (jnp.dot(tril, hi, preferred_element_type=F32)
         + jnp.dot(tril, mid, preferred_element_type=F32)
         + jnp.dot(tril, lo, preferred_element_type=F32))

    nchunk = HG_BLOCK // HG_CHUNK
    bl = [b[(j + 1) * HG_CHUNK - 1:(j + 1) * HG_CHUNK, :] for j in range(nchunk)]
    zero = jnp.zeros_like(bl[0])
    b_tot = bl[0] + bl[1] + bl[2] + bl[3]
    q_dec = q * jnp.exp(b)
    k_inv = k * jnp.exp(-b)
    q_mid = q_dec * _chunk_rows([jnp.exp(zero), jnp.exp(-bl[1]), jnp.exp(zero), jnp.exp(bl[2])], d)
    k_mid = k_inv * _chunk_rows([jnp.exp(bl[0] + bl[1]), jnp.exp(bl[1]), jnp.exp(zero), jnp.exp(zero)], d)
    q_blk = q_dec * _chunk_rows([jnp.exp(zero), jnp.exp(bl[0]), jnp.exp(bl[0] + bl[1]),
                                 jnp.exp(bl[0] + bl[1] + bl[2])], d)
    k_blk = k_inv * _chunk_rows([jnp.exp(b_tot), jnp.exp(b_tot - bl[0]), jnp.exp(bl[2] + bl[3]),
                                 jnp.exp(bl[3])], d)
    dec = jnp.exp(b_tot)

    nt = (((1,), (1,)), ((), ()))
    tn = (((0,), (0,)), ((), ()))
    for hd in range(N_HEADS):
        sl = slice(hd * HEAD_DIM, (hd + 1) * HEAD_DIM)
        v = p_ref[:, 2 * d + hd * HEAD_DIM:2 * d + (hd + 1) * HEAD_DIM].astype(BF16)
        gate = p_ref[:, 3 * d + hd * HEAD_DIM:3 * d + (hd + 1) * HEAD_DIM]
        a_diag = lax.dot_general(q_dec[:, sl].astype(BF16), k_inv[:, sl].astype(BF16), nt,
                                 preferred_element_type=F32)
        a_off = lax.dot_general(q_mid[:, sl].astype(BF16), k_mid[:, sl].astype(BF16), nt,
                                preferred_element_type=F32)
        a = jnp.where(diag_mask, a_diag, jnp.where(off_mask, a_off, 0.0))
        s0 = state[hd]
        o = (jnp.dot(a.astype(BF16), v, preferred_element_type=F32)
             + jnp.dot(q_blk[:, sl].astype(BF16), s0.astype(BF16), preferred_element_type=F32))
        dec_col = jnp.transpose(jnp.broadcast_to(dec[:, sl], (HEAD_DIM, HEAD_DIM)))
        state[hd] = dec_col * s0 + lax.dot_general(k_blk[:, sl].astype(BF16), v, tn,
                                                   preferred_element_type=F32)
        o = _rms(o) * ng_ref[:, sl]
        o_ref[:, sl] = (o * _silu(gate)).astype(o_ref.dtype)


def _hgrn_mix(proj, lower_bounds, norm_g, layer_idx):
    bsz, seq, n = proj.shape
    d = D_MODEL
    n_layers = lower_bounds.shape[0]
    return pl.pallas_call(
        functools.partial(_hgrn_kernel, layer_idx=layer_idx),
        out_shape=jax.ShapeDtypeStruct((bsz, seq, d), BF16),
        grid=(bsz, seq // HG_BLOCK),
        in_specs=[
            pl.BlockSpec((None, HG_BLOCK, n), lambda b, i: (b, i, 0)),
            pl.BlockSpec((n_layers, d), lambda b, i: (0, 0)),
            pl.BlockSpec((1, d), lambda b, i: (0, 0)),
        ],
        out_specs=pl.BlockSpec((None, HG_BLOCK, d), lambda b, i: (b, i, 0)),
        scratch_shapes=[pltpu.VMEM((N_HEADS, HEAD_DIM, HEAD_DIM), F32)],
        compiler_params=_params(("parallel", "arbitrary"), VMEM_LIMIT_V7X),
        name="hgrn2_recurrence",
    )(proj, lower_bounds, norm_g)


def _attn_kernel(q_ref, kp_ref, kc_ref, vp_ref, vc_ref, o_ref, lse_ref, *, tq):
    first_tile = pl.program_id(2) == 0
    qi = lax.broadcasted_iota(jnp.int32, (ATT_BLOCK, 2 * ATT_BLOCK), 0)
    ki = lax.broadcasted_iota(jnp.int32, (ATT_BLOCK, 2 * ATT_BLOCK), 1)
    band = (ki >= qi) & (ki <= qi + ATT_BLOCK)
    lane = lax.broadcasted_iota(jnp.int32, (ATT_BLOCK, HEAD_DIM), 1)
    nt = (((1,), (1,)), ((), ()))
    for n in range(tq // ATT_BLOCK):
        rows = slice(n * ATT_BLOCK, (n + 1) * ATT_BLOCK)
        if n == 0:
            k_prev, v_prev = kp_ref, vp_ref
            prev_rows = slice(0, ATT_BLOCK)
            valid = band & (jnp.logical_not(first_tile) | (ki >= ATT_BLOCK))
        else:
            k_prev, v_prev = kc_ref, vc_ref
            prev_rows = slice((n - 1) * ATT_BLOCK, n * ATT_BLOCK)
            valid = band
        lse_tile = jnp.zeros((ATT_BLOCK, HEAD_DIM), F32)
        for hd in range(N_HEADS):
            sl = slice(hd * HEAD_DIM, (hd + 1) * HEAD_DIM)
            kw = jnp.concatenate([k_prev[prev_rows, sl], kc_ref[rows, sl]], axis=0)
            vw = jnp.concatenate([v_prev[prev_rows, sl], vc_ref[rows, sl]], axis=0)
            s = lax.dot_general(q_ref[rows, sl], kw, nt, preferred_element_type=F32)
            s = jnp.where(valid, s, -jnp.inf)
            m = jnp.max(s, axis=-1, keepdims=True)
            p = jnp.exp(s - m)
            z = jnp.sum(p, axis=-1, keepdims=True)
            o = jnp.dot(p.astype(BF16), vw, preferred_element_type=F32) / z
            o_ref[hd, rows, :] = o
            lse_tile = jnp.where(lane == hd, m + jnp.log(z), lse_tile)
        lse_ref[rows, :] = lse_tile


def _dilated_attention(qkv):
    bsz, dilation, sub, _ = qkv.shape
    d = D_MODEL
    tq = min(sub, 512)
    blk_per_tile = tq // ATT_BLOCK

    def cur(which):
        return lambda b, r, i: (b, r, i, which)

    def prev(which):
        return lambda b, r, i: (b, r, jnp.maximum(i * blk_per_tile - 1, 0), which)

    return pl.pallas_call(
        functools.partial(_attn_kernel, tq=tq),
        out_shape=(jax.ShapeDtypeStruct((bsz, dilation, N_HEADS, sub, HEAD_DIM), F32),
                   jax.ShapeDtypeStruct((bsz, dilation, sub, HEAD_DIM), F32)),
        grid=(bsz, dilation, sub // tq),
        in_specs=[
            pl.BlockSpec((None, None, tq, d), cur(0)),
            pl.BlockSpec((None, None, ATT_BLOCK, d), prev(1)),
            pl.BlockSpec((None, None, tq, d), cur(1)),
            pl.BlockSpec((None, None, ATT_BLOCK, d), prev(2)),
            pl.BlockSpec((None, None, tq, d), cur(2)),
        ],
        out_specs=(pl.BlockSpec((None, None, N_HEADS, tq, HEAD_DIM), lambda b, r, i: (b, r, 0, i, 0)),
                   pl.BlockSpec((None, None, tq, HEAD_DIM), cur(0))),
        compiler_params=_params(("parallel", "parallel", "parallel"), VMEM_LIMIT_V7X),
        name="dilated_attention",
    )(qkv, qkv, qkv, qkv, qkv)


def _residual_update(y, h_ref, mod_ref, g_ref, gate_row, o_ref):
    o_ref[...] = h_ref[...] + (1.0 + mod_ref[gate_row:gate_row + 1, :]) * (_rms(y) * g_ref[...])


def _outproj_kernel(x_ref, w_ref, h_ref, mod_ref, g_ref, o_ref):
    y = jnp.dot(x_ref[...], w_ref[...], preferred_element_type=F32)
    _residual_update(y, h_ref, mod_ref, g_ref, 2, o_ref)


def _token_rows(src_ref, dst_ref):
    dilation, n = src_ref.shape[0], src_ref.shape[-2]
    for r in range(dilation):
        if len(src_ref.shape) == 3:
            dst_ref[pl.ds(r, n, stride=dilation), :] = src_ref[r]
        else:
            for hd in range(src_ref.shape[1]):
                dst_ref[hd, pl.ds(r, n, stride=dilation), :] = src_ref[r, hd]


def _attn_outproj_kernel(o1_ref, o2_ref, o3_ref, l1_ref, l2_ref, l3_ref, w_ref, h_ref, mod_ref,
                         g_ref, o_ref, o2_scr, o3_scr, l2_scr, l3_scr):
    _token_rows(o2_ref, o2_scr)
    _token_rows(o3_ref, o3_scr)
    _token_rows(l2_ref, l2_scr)
    _token_rows(l3_ref, l3_scr)
    l1, l2, l3 = l1_ref[0], l2_scr[...], l3_scr[...]
    m = jnp.maximum(jnp.maximum(l1, l2), l3)
    e1, e2, e3 = jnp.exp(l1 - m), jnp.exp(l2 - m), jnp.exp(l3 - m)
    z = e1 + e2 + e3
    w1, w2, w3 = e1 / z, e2 / z, e3 / z
    pieces = []
    for hd in range(N_HEADS):
        o = (w1[:, hd:hd + 1] * o1_ref[0, hd] + w2[:, hd:hd + 1] * o2_scr[hd]
             + w3[:, hd:hd + 1] * o3_scr[hd])
        pieces.append(o.astype(BF16))
    x = jnp.concatenate(pieces, axis=1)
    y = jnp.dot(x, w_ref[...], preferred_element_type=F32)
    _residual_update(y, h_ref, mod_ref, g_ref, 2, o_ref)


def _out_projection(x, w, h, mod_l, gain, tm=512):
    bsz, seq, d = h.shape
    tok = lambda b, i: (b, i, 0)
    return pl.pallas_call(
        _outproj_kernel,
        out_shape=jax.ShapeDtypeStruct(h.shape, F32),
        grid=(bsz, seq // tm),
        in_specs=[
            pl.BlockSpec((None, tm, x.shape[-1]), tok),
            pl.BlockSpec(w.shape, lambda b, i: (0, 0)),
            pl.BlockSpec((None, tm, d), tok),
            pl.BlockSpec((None, 6, d), lambda b, i: (b, 0, 0)),
            pl.BlockSpec((1, d), lambda b, i: (0, 0)),
        ],
        out_specs=pl.BlockSpec((None, tm, d), tok),
        compiler_params=_params(("parallel", "parallel"), VMEM_LIMIT_V7X),
        name="out_projection",
    )(x, w, h, mod_l, gain)


def _attn_out_projection(outs, w, h, mod_l, gain, tm=512):
    bsz, seq, d = h.shape
    tok = lambda b, i: (b, i, 0)
    res = lambda b, i: (b, 0, i, 0)
    o_specs = [pl.BlockSpec((None, dil, N_HEADS, tm // dil, HEAD_DIM), lambda b, i: (b, 0, 0, i, 0))
               for dil in DILATIONS]
    l_specs = [pl.BlockSpec((None, dil, tm // dil, HEAD_DIM), res) for dil in DILATIONS]
    return pl.pallas_call(
        _attn_outproj_kernel,
        out_shape=jax.ShapeDtypeStruct(h.shape, F32),
        grid=(bsz, seq // tm),
        in_specs=o_specs + l_specs + [
            pl.BlockSpec(w.shape, lambda b, i: (0, 0)),
            pl.BlockSpec((None, tm, d), tok),
            pl.BlockSpec((None, 6, d), lambda b, i: (b, 0, 0)),
            pl.BlockSpec((1, d), lambda b, i: (0, 0)),
        ],
        out_specs=pl.BlockSpec((None, tm, d), tok),
        scratch_shapes=[pltpu.VMEM((N_HEADS, tm, HEAD_DIM), F32), pltpu.VMEM((N_HEADS, tm, HEAD_DIM), F32),
                        pltpu.VMEM((tm, HEAD_DIM), F32), pltpu.VMEM((tm, HEAD_DIM), F32)],
        compiler_params=_params(("parallel", "parallel"), VMEM_LIMIT_V7X),
        name="attn_out_projection",
    )(*[o for o, _ in outs], *[l for _, l in outs], w, h, mod_l, gain)


FFN_CHUNK = 256


def _ffn_kernel(h_ref, mod_ref, g_in_ref, g_out_ref, w_in_ref, w_out_ref, o_ref):
    u = _normed_input(h_ref, mod_ref, g_in_ref, 3).astype(BF16)
    acc = jnp.zeros(h_ref.shape, F32)
    for c0 in range(0, D_FF, FFN_CHUNK):
        gate = jnp.dot(u, w_in_ref[:, c0:c0 + FFN_CHUNK], preferred_element_type=F32)
        up = jnp.dot(u, w_in_ref[:, D_FF + c0:D_FF + c0 + FFN_CHUNK], preferred_element_type=F32)
        act = (_silu(gate) * up).astype(BF16)
        acc = acc + jnp.dot(act, w_out_ref[c0:c0 + FFN_CHUNK, :], preferred_element_type=F32)
    _residual_update(acc, h_ref, mod_ref, g_out_ref, 5, o_ref)


def _ffn(h, mod_l, g_in, g_out, w_in, w_out, tm=512):
    bsz, seq, d = h.shape
    tok = lambda b, i: (b, i, 0)
    const = lambda b, i: (0, 0)
    return pl.pallas_call(
        _ffn_kernel,
        out_shape=jax.ShapeDtypeStruct(h.shape, F32),
        grid=(bsz, seq // tm),
        in_specs=[
            pl.BlockSpec((None, tm, d), tok),
            pl.BlockSpec((None, 6, d), lambda b, i: (b, 0, 0)),
            pl.BlockSpec((1, d), const),
            pl.BlockSpec((1, d), const),
            pl.BlockSpec(w_in.shape, const, pipeline_mode=pl.Buffered(1)),
            pl.BlockSpec(w_out.shape, const, pipeline_mode=pl.Buffered(1)),
        ],
        out_specs=pl.BlockSpec((None, tm, d), tok),
        compiler_params=_params(("parallel", "parallel"), VMEM_LIMIT_V7X),
        name="swiglu_ffn",
    )(h, mod_l, g_in, g_out, w_in, w_out)


def kernel(x, c, positions, ada_w, ada_b, norm_g, hgrn_w_in, hgrn_lower_bounds, hgrn_norm_g,
           hgrn_w_out, attn_w_in, attn_w_out, ffn_w_in, ffn_w_out):
    depth = ada_w.shape[0]
    bsz, seq, d = x.shape
    mod = _modulation(c, ada_w, ada_b).reshape(depth, bsz, 6, d)
    rope = _rope_tables(positions)
    h = x
    for layer in range(depth):
        mod_l = mod[layer]
        gains = norm_g[layer].reshape(4, 1, d)
        idx = layer // 2
        if layer % 2 == 0:
            proj = _in_projection(h, mod_l, gains[0], hgrn_w_in[idx].astype(BF16))
            mixed = _hgrn_mix(proj, hgrn_lower_bounds, hgrn_norm_g[idx].reshape(1, d), idx)
            h = _out_projection(mixed, hgrn_w_out[idx].astype(BF16), h, mod_l, gains[1])
        else:
            w_in = _rope_layout(attn_w_in[idx]).astype(BF16)
            outs = [_dilated_attention(_attn_in_projection(h, mod_l, gains[0], rope, w_in, g, dil))
                    for g, dil in enumerate(DILATIONS)]
            h = _attn_out_projection(outs, attn_w_out[idx].astype(BF16), h, mod_l, gains[1])
        h = _ffn(h, mod_l, gains[2], gains[3], ffn_w_in[layer].astype(BF16),
                 ffn_w_out[layer].astype(BF16))
    return h
```

```python
import functools

import jax
import jax.numpy as jnp
from jax import lax
from jax.experimental import pallas as pl
from jax.experimental.pallas import tpu as pltpu

F32 = jnp.float32
BF16 = jnp.bfloat16

D_MODEL = 1024
HEAD_DIM = 128
N_HEADS = D_MODEL // HEAD_DIM
HG_CHUNK = 32
HG_BLOCK = 128
ATT_BLOCK = 128
DILATIONS = (1, 4, 16)
ROPE_DIM = HEAD_DIM // 4
ROPE_HALF = ROPE_DIM // 2
ROPE_THETA = 500000.0
D_FF = -(-(8 * D_MODEL) // (3 * 256)) * 256
NORM_EPS = 1e-6
MXU_COLS_V7X = 256
VMEM_LIMIT_V7X = 56 * 1024 * 1024


def _params(sem, vmem=None):
    return pltpu.CompilerParams(dimension_semantics=sem, vmem_limit_bytes=vmem)


def _rms(x):
    return x * lax.rsqrt(jnp.mean(x * x, axis=-1, keepdims=True) + NORM_EPS)


def _silu(x):
    return x * jax.nn.sigmoid(x)


def _normed_input(h_ref, mod_ref, g_ref, shift_row):
    y = _rms(h_ref[...]) * g_ref[...]
    return y * (1.0 + mod_ref[shift_row + 1:shift_row + 2, :]) + mod_ref[shift_row:shift_row + 1, :]


def _residual_update(y, h_ref, mod_ref, g_ref, gate_row, o_ref):
    o_ref[...] = h_ref[...] + (1.0 + mod_ref[gate_row:gate_row + 1, :]) * (_rms(y) * g_ref[...])


def _mod_kernel(c_ref, w_ref, b_ref, o_ref):
    cond = _silu(c_ref[...])
    o_ref[...] = jnp.dot(cond, w_ref[...], preferred_element_type=F32,
                         precision=lax.Precision.HIGHEST) + b_ref[...]


def _modulation(c, ada_w, ada_b):
    depth, d, n = ada_w.shape
    bsz = c.shape[0]
    tn = 1536
    return pl.pallas_call(
        _mod_kernel,
        out_shape=jax.ShapeDtypeStruct((depth, bsz, n), F32),
        grid=(depth, n // tn),
        in_specs=[
            pl.BlockSpec((bsz, d), lambda l, j: (0, 0)),
            pl.BlockSpec((None, d, tn), lambda l, j: (l, 0, j)),
            pl.BlockSpec((None, 1, tn), lambda l, j: (l, 0, j)),
        ],
        out_specs=pl.BlockSpec((None, bsz, tn), lambda l, j: (l, 0, j)),
        compiler_params=_params(("parallel", "parallel")),
        name="adaln_modulation",
    )(c, ada_w, ada_b.reshape(depth, 1, n))


def _rope_kernel(pos_ref, invf_ref, sgn_ref, cos_ref, sin_ref):
    ang = pos_ref[...].astype(F32) * invf_ref[...]
    cos_ref[...] = jnp.cos(ang)
    sin_ref[...] = jnp.sin(ang) * sgn_ref[...]


def _rope_tables(positions):
    bsz, seq = positions.shape
    ts = 1024
    inv_freq = ROPE_THETA ** (-jnp.arange(0, ROPE_DIM, 2, dtype=F32) / ROPE_DIM)
    zeros = jnp.zeros((HEAD_DIM // 2 - ROPE_HALF,), F32)
    invf = jnp.concatenate([inv_freq, zeros, inv_freq, zeros]).reshape(1, HEAD_DIM)
    sgn = jnp.concatenate([-jnp.ones((ROPE_HALF,), F32), zeros,
                           jnp.ones((ROPE_HALF,), F32), zeros]).reshape(1, HEAD_DIM)
    out = jax.ShapeDtypeStruct((bsz, seq, HEAD_DIM), F32)
    return pl.pallas_call(
        _rope_kernel,
        out_shape=(out, out),
        grid=(bsz, seq // ts),
        in_specs=[
            pl.BlockSpec((None, ts, 1), lambda b, i: (b, i, 0)),
            pl.BlockSpec((1, HEAD_DIM), lambda b, i: (0, 0)),
            pl.BlockSpec((1, HEAD_DIM), lambda b, i: (0, 0)),
        ],
        out_specs=(pl.BlockSpec((None, ts, HEAD_DIM), lambda b, i: (b, i, 0)),
                   pl.BlockSpec((None, ts, HEAD_DIM), lambda b, i: (b, i, 0))),
        compiler_params=_params(("parallel", "parallel")),
        name="rope_tables",
    )(positions.reshape(bsz, seq, 1), invf, sgn)


def _rope_layout(w):
    d = w.shape[0]
    w = w.reshape(d, len(DILATIONS), 3, N_HEADS, HEAD_DIM)
    qk = w[:, :, :2]
    qk = jnp.concatenate([qk[..., :ROPE_HALF], qk[..., ROPE_DIM:HEAD_DIM // 2 + ROPE_HALF],
                          qk[..., ROPE_HALF:ROPE_DIM], qk[..., HEAD_DIM // 2 + ROPE_HALF:]], axis=-1)
    return jnp.concatenate([qk, w[:, :, 2:]], axis=2).reshape(d, -1)


def _scale_chunks(x, vals):
    parts = []
    for j, v in enumerate(vals):
        piece = x[j * HG_CHUNK:(j + 1) * HG_CHUNK, :]
        parts.append(piece if v is None else piece * v)
    return jnp.concatenate(parts, axis=0)


def _hgrn_block(proj_scr, mix_scr, rows, lb, ng_ref, state, masks):
    d = D_MODEL
    diag_mask, off_mask, tril3 = masks
    sig = jax.nn.sigmoid(proj_scr[rows, d:2 * d])
    log_f = jnp.log(lb + (1.0 - lb) * sig)
    k = (1.0 - lb) * (1.0 - sig)
    q = _silu(proj_scr[rows, 0:d])
    hi = log_f.astype(BF16)
    rem = log_f - hi.astype(F32)
    mid = rem.astype(BF16)
    lo = (rem - mid.astype(F32)).astype(BF16)
    b = jnp.dot(tril3, jnp.concatenate([hi, mid, lo], axis=0), preferred_element_type=F32)
    nchunk = HG_BLOCK // HG_CHUNK
    bl = [b[(j + 1) * HG_CHUNK - 1:(j + 1) * HG_CHUNK, :] for j in range(nchunk)]
    b_tot = bl[0] + bl[1] + bl[2] + bl[3]
    q_dec = q * jnp.exp(b)
    k_inv = k * jnp.exp(-b)
    q_mid = _scale_chunks(q_dec, [None, jnp.exp(-bl[1]), None, jnp.exp(bl[2])]).astype(BF16)
    k_mid = _scale_chunks(k_inv, [jnp.exp(bl[0] + bl[1]), jnp.exp(bl[1]), None, None]).astype(BF16)
    q_blk = _scale_chunks(q_dec, [None, jnp.exp(bl[0]), jnp.exp(bl[0] + bl[1]),
                                  jnp.exp(bl[0] + bl[1] + bl[2])]).astype(BF16)
    k_blk = _scale_chunks(k_inv, [jnp.exp(b_tot), jnp.exp(b_tot - bl[0]), jnp.exp(bl[2] + bl[3]),
                                  jnp.exp(bl[3])]).astype(BF16)
    q_dec = q_dec.astype(BF16)
    k_inv = k_inv.astype(BF16)
    dec = jnp.exp(b_tot)

    nt = (((1,), (1,)), ((), ()))
    tn = (((0,), (0,)), ((), ()))
    for hd in range(N_HEADS):
        sl = slice(hd * HEAD_DIM, (hd + 1) * HEAD_DIM)
        v = proj_scr[rows, 2 * d + hd * HEAD_DIM:2 * d + (hd + 1) * HEAD_DIM].astype(BF16)
        gate = proj_scr[rows, 3 * d + hd * HEAD_DIM:3 * d + (hd + 1) * HEAD_DIM]
        a_diag = lax.dot_general(q_dec[:, sl], k_inv[:, sl], nt, preferred_element_type=F32)
        a_off = lax.dot_general(q_mid[:, sl], k_mid[:, sl], nt, preferred_element_type=F32)
        a = jnp.where(diag_mask, a_diag, jnp.where(off_mask, a_off, 0.0)).astype(BF16)
        s0 = state[hd]
        o = jnp.dot(jnp.concatenate([a, q_blk[:, sl]], axis=1),
                    jnp.concatenate([v, s0.astype(BF16)], axis=0), preferred_element_type=F32)
        dec_col = jnp.transpose(jnp.broadcast_to(dec[:, sl], (HEAD_DIM, HEAD_DIM)))
        state[hd] = dec_col * s0 + lax.dot_general(k_blk[:, sl], v, tn, preferred_element_type=F32)
        mix_scr[rows, sl] = (_rms(o) * ng_ref[:, sl] * _silu(gate)).astype(BF16)


def _hgrn_layer_kernel(h_ref, mod_ref, g_in_ref, g_out_ref, lb_ref, ng_ref, w_in_ref, w_out_ref,
                       o_ref, proj_scr, mix_scr, state, *, layer_idx):
    @pl.when(pl.program_id(1) == 0)
    def _():
        state[...] = jnp.zeros_like(state)

    u = _normed_input(h_ref, mod_ref, g_in_ref, 0).astype(BF16)
    proj_scr[...] = jnp.dot(u, w_in_ref[...], preferred_element_type=F32)

    lbr = lb_ref[...]
    e = jnp.exp(lbr - jnp.max(lbr, axis=0, keepdims=True))
    sm = e / jnp.sum(e, axis=0, keepdims=True)
    csum = sm[0:1, :]
    first = csum
    for r in range(1, layer_idx + 1):
        csum = csum + sm[r:r + 1, :]
    lb = csum - first

    ti = lax.broadcasted_iota(jnp.int32, (HG_BLOCK, HG_BLOCK), 0)
    si = lax.broadcasted_iota(jnp.int32, (HG_BLOCK, HG_BLOCK), 1)
    diag_mask = ((ti // HG_CHUNK) == (si // HG_CHUNK)) & (si <= ti)
    off_mask = (si // HG_CHUNK) < (ti // HG_CHUNK)
    tril = jnp.where(diag_mask, 1.0, 0.0).astype(BF16)
    masks = (diag_mask, off_mask, jnp.concatenate([tril, tril, tril], axis=1))

    def block(blk, carry):
        rows = pl.ds(pl.multiple_of(blk * HG_BLOCK, HG_BLOCK), HG_BLOCK)
        _hgrn_block(proj_scr, mix_scr, rows, lb, ng_ref, state, masks)
        return carry

    lax.fori_loop(0, h_ref.shape[0] // HG_BLOCK, block, 0)
    y = jnp.dot(mix_scr[...], w_out_ref[...], preferred_element_type=F32)
    _residual_update(y, h_ref, mod_ref, g_out_ref, 2, o_ref)


def _hgrn_layer(h, mod_l, g_in, g_out, lower_bounds, norm_g, w_in, w_out, layer_idx, tm=512):
    bsz, seq, d = h.shape
    tok = lambda b, i: (b, i, 0)
    const = lambda b, i: (0, 0)
    return pl.pallas_call(
        functools.partial(_hgrn_layer_kernel, layer_idx=layer_idx),
        out_shape=jax.ShapeDtypeStruct(h.shape, F32),
        grid=(bsz, seq // tm),
        in_specs=[
            pl.BlockSpec((None, tm, d), tok),
            pl.BlockSpec((None, 6, d), lambda b, i: (b, 0, 0)),
            pl.BlockSpec((1, d), const),
            pl.BlockSpec((1, d), const),
            pl.BlockSpec(lower_bounds.shape, const),
            pl.BlockSpec((1, d), const),
            pl.BlockSpec(w_in.shape, const, pipeline_mode=pl.Buffered(1)),
            pl.BlockSpec(w_out.shape, const, pipeline_mode=pl.Buffered(1)),
        ],
        out_specs=pl.BlockSpec((None, tm, d), tok),
        scratch_shapes=[pltpu.VMEM((tm, 4 * d), F32), pltpu.VMEM((tm, d), BF16),
                        pltpu.VMEM((N_HEADS, HEAD_DIM, HEAD_DIM), F32)],
        compiler_params=_params(("parallel", "arbitrary"), VMEM_LIMIT_V7X),
        name="hgrn2_layer",
    )(h, mod_l, g_in, g_out, lower_bounds, norm_g, w_in, w_out)


def _residue_rows(ref, dilation):
    if dilation == 1:
        return ref[...]
    n = ref.shape[0] // dilation
    return jnp.concatenate([ref[pl.ds(r, n, stride=dilation), :] for r in range(dilation)], axis=0)


def _attn_inproj_kernel(h_ref, mod_ref, g_ref, cos_ref, sin_ref, w_ref, o_ref,
                        u_slab, u_perm, cos_perm, sin_perm, cos_eff, sin_eff):
    j = pl.program_id(2)
    kind = j % 3
    n_slabs = D_MODEL // HEAD_DIM

    @pl.when(j == 0)
    def _():
        u = _normed_input(h_ref, mod_ref, g_ref, 0)
        u_perm[...] = u.astype(BF16)
        for c in range(n_slabs):
            u_slab[c] = u[:, c * HEAD_DIM:(c + 1) * HEAD_DIM]
        cos_perm[...] = cos_ref[...]
        sin_perm[...] = sin_ref[...]

    for group, dilation in enumerate(DILATIONS):
        if dilation == 1:
            continue

        @pl.when(j == 3 * group)
        def _():
            for c in range(n_slabs):
                u_perm[:, c * HEAD_DIM:(c + 1) * HEAD_DIM] = (
                    _residue_rows(u_slab.at[c], dilation).astype(BF16))
            cos_perm[...] = _residue_rows(cos_ref, dilation)
            sin_perm[...] = _residue_rows(sin_ref, dilation)

    scale = jnp.where(kind == 0, HEAD_DIM ** -0.5, 1.0).astype(F32)
    is_v = kind == 2
    cos_eff[...] = jnp.where(is_v, 1.0, cos_perm[...] * scale)
    sin_eff[...] = jnp.where(is_v, 0.0, sin_perm[...] * scale)
    for c0 in range(0, D_MODEL, MXU_COLS_V7X):
        acc = jnp.dot(u_perm[...], w_ref[:, c0:c0 + MXU_COLS_V7X], preferred_element_type=F32)
        for c1 in range(0, MXU_COLS_V7X, HEAD_DIM):
            t = acc[:, c1:c1 + HEAD_DIM]
            partner = pltpu.roll(t, HEAD_DIM // 2, axis=1)
            o_ref[:, c0 + c1:c0 + c1 + HEAD_DIM] = (
                t * cos_eff[...] + partner * sin_eff[...]).astype(o_ref.dtype)


def _attn_in_projection(h, mod_l, gain, rope, w, tm=1024):
    bsz, seq, d = h.shape
    tok = lambda b, i, j: (b, i, 0)
    return pl.pallas_call(
        _attn_inproj_kernel,
        out_shape=jax.ShapeDtypeStruct((len(DILATIONS), bsz, seq // tm, tm, 3 * d), BF16),
        grid=(bsz, seq // tm, 3 * len(DILATIONS)),
        in_specs=[
            pl.BlockSpec((None, tm, d), tok),
            pl.BlockSpec((None, 6, d), lambda b, i, j: (b, 0, 0)),
            pl.BlockSpec((1, d), lambda b, i, j: (0, 0)),
            pl.BlockSpec((None, tm, HEAD_DIM), tok),
            pl.BlockSpec((None, tm, HEAD_DIM), tok),
            pl.BlockSpec((d, d), lambda b, i, j: (0, j)),
        ],
        out_specs=pl.BlockSpec((None, None, None, tm, d), lambda b, i, j: (j // 3, b, i, 0, j % 3)),
        scratch_shapes=[pltpu.VMEM((d // HEAD_DIM, tm, HEAD_DIM), F32), pltpu.VMEM((tm, d), BF16),
                        pltpu.VMEM((tm, HEAD_DIM), F32), pltpu.VMEM((tm, HEAD_DIM), F32),
                        pltpu.VMEM((tm, HEAD_DIM), F32), pltpu.VMEM((tm, HEAD_DIM), F32)],
        compiler_params=_params(("parallel", "parallel", "arbitrary"), VMEM_LIMIT_V7X),
        name="attn_in_projection",
    )(h, mod_l, gain, rope[0], rope[1], w)


def _block_rows(ref, blk, sl):
    if len(ref.shape) == 2:
        return ref[blk * ATT_BLOCK:(blk + 1) * ATT_BLOCK, sl]
    n = ref.shape[1]
    if n >= ATT_BLOCK:
        start = blk * ATT_BLOCK
        return ref[start // n, start % n:start % n + ATT_BLOCK, sl]
    per = ATT_BLOCK // n
    return jnp.concatenate([ref[blk * per + t, :, sl] for t in range(per)], axis=0)


def _attn_kernel(q_ref, kp_ref, kc_ref, vp_ref, vc_ref, o_ref, lse_ref, *, tq):
    first_tile = pl.program_id(2) == 0
    qi = lax.broadcasted_iota(jnp.int32, (ATT_BLOCK, 2 * ATT_BLOCK), 0)
    ki = lax.broadcasted_iota(jnp.int32, (ATT_BLOCK, 2 * ATT_BLOCK), 1)
    band = (ki >= qi) & (ki <= qi + ATT_BLOCK)
    lane = lax.broadcasted_iota(jnp.int32, (ATT_BLOCK, HEAD_DIM), 1)
    nt = (((1,), (1,)), ((), ()))
    for n in range(tq // ATT_BLOCK):
        rows = slice(n * ATT_BLOCK, (n + 1) * ATT_BLOCK)
        if n == 0:
            k_prev, v_prev, prev_blk = kp_ref, vp_ref, 0
            valid = band & (jnp.logical_not(first_tile) | (ki >= ATT_BLOCK))
        else:
            k_prev, v_prev, prev_blk = kc_ref, vc_ref, n - 1
            valid = band
        lse_tile = jnp.zeros((ATT_BLOCK, HEAD_DIM), F32)
        for hd in range(N_HEADS):
            sl = slice(hd * HEAD_DIM, (hd + 1) * HEAD_DIM)
            kw = jnp.concatenate([_block_rows(k_prev, prev_blk, sl), _block_rows(kc_ref, n, sl)], axis=0)
            vw = jnp.concatenate([_block_rows(v_prev, prev_blk, sl), _block_rows(vc_ref, n, sl)], axis=0)
            s = lax.dot_general(_block_rows(q_ref, n, sl), kw, nt, preferred_element_type=F32)
            s = jnp.where(valid, s, -jnp.inf)
            m = jnp.max(s, axis=-1, keepdims=True)
            p = jnp.exp(s - m)
            z = jnp.sum(p, axis=-1, keepdims=True)
            o = jnp.dot(p.astype(BF16), vw, preferred_element_type=F32) / z
            o_ref[hd, rows, :] = o
            lse_tile = jnp.where(lane == hd, m + jnp.log(z), lse_tile)
        lse_ref[rows, :] = lse_tile


def _dilated_attention(qkv, group, dilation):
    _, bsz, n_tiles, tm, n3 = qkv.shape
    d = D_MODEL
    n = tm // dilation
    sub = n_tiles * n
    tq = min(sub, 512)
    view = qkv.reshape(len(DILATIONS), bsz, n_tiles, dilation, n, n3)

    def prev_rows(which):
        if n >= ATT_BLOCK:
            def index(b, r, i):
                row0 = jnp.maximum(i * tq - ATT_BLOCK, 0)
                return (group, b, row0 // n, r, (row0 % n) // ATT_BLOCK, which)
            return pl.BlockSpec((None, None, None, None, ATT_BLOCK, d), index)
        per = ATT_BLOCK // n
        return pl.BlockSpec(
            (None, None, per, None, n, d),
            lambda b, r, i: (group, b, jnp.maximum(i * (tq // ATT_BLOCK) - 1, 0), r, 0, which))

    def cur_rows(which):
        if n >= tq:
            per = n // tq
            return pl.BlockSpec((None, None, None, None, tq, d),
                                lambda b, r, i: (group, b, i // per, r, i % per, which))
        return pl.BlockSpec((None, None, tq // n, None, n, d),
                            lambda b, r, i: (group, b, i, r, 0, which))

    return pl.pallas_call(
        functools.partial(_attn_kernel, tq=tq),
        out_shape=(jax.ShapeDtypeStruct((bsz, dilation, N_HEADS, sub, HEAD_DIM), F32),
                   jax.ShapeDtypeStruct((bsz, dilation, sub, HEAD_DIM), F32)),
        grid=(bsz, dilation, sub // tq),
        in_specs=[cur_rows(0), prev_rows(1), cur_rows(1), prev_rows(2), cur_rows(2)],
        out_specs=(pl.BlockSpec((None, None, N_HEADS, tq, HEAD_DIM), lambda b, r, i: (b, r, 0, i, 0)),
                   pl.BlockSpec((None, None, tq, HEAD_DIM), lambda b, r, i: (b, r, i, 0))),
        compiler_params=_params(("parallel", "parallel", "parallel"), VMEM_LIMIT_V7X),
        name="dilated_attention",
    )(view, view, view, view, view)


def _token_rows(src_ref, dst_ref):
    dilation, n = src_ref.shape[0], src_ref.shape[-2]
    for r in range(dilation):
        if len(src_ref.shape) == 3:
            dst_ref[pl.ds(r, n, stride=dilation), :] = src_ref[r]
        else:
            for hd in range(src_ref.shape[1]):
                dst_ref[hd, pl.ds(r, n, stride=dilation), :] = src_ref[r, hd]


def _attn_outproj_kernel(o1_ref, o2_ref, o3_ref, l1_ref, l2_ref, l3_ref, w_ref, h_ref, mod_ref,
                         g_ref, o_ref, o2_scr, o3_scr, l2_scr, l3_scr):
    _token_rows(o2_ref, o2_scr)
    _token_rows(o3_ref, o3_scr)
    _token_rows(l2_ref, l2_scr)
    _token_rows(l3_ref, l3_scr)
    l1, l2, l3 = l1_ref[0], l2_scr[...], l3_scr[...]
    m = jnp.maximum(jnp.maximum(l1, l2), l3)
    e1, e2, e3 = jnp.exp(l1 - m), jnp.exp(l2 - m), jnp.exp(l3 - m)
    z = e1 + e2 + e3
    w1, w2, w3 = e1 / z, e2 / z, e3 / z
    pieces = []
    for hd in range(N_HEADS):
        o = (w1[:, hd:hd + 1] * o1_ref[0, hd] + w2[:, hd:hd + 1] * o2_scr[hd]
             + w3[:, hd:hd + 1] * o3_scr[hd])
        pieces.append(o.astype(BF16))
    x = jnp.concatenate(pieces, axis=1)
    y = jnp.dot(x, w_ref[...], preferred_element_type=F32)
    _residual_update(y, h_ref, mod_ref, g_ref, 2, o_ref)


def _attn_out_projection(outs, w, h, mod_l, gain, tm=512):
    bsz, seq, d = h.shape
    tok = lambda b, i: (b, i, 0)
    o_specs = [pl.BlockSpec((None, dil, N_HEADS, tm // dil, HEAD_DIM), lambda b, i: (b, 0, 0, i, 0))
               for dil in DILATIONS]
    l_specs = [pl.BlockSpec((None, dil, tm // dil, HEAD_DIM), lambda b, i: (b, 0, i, 0))
               for dil in DILATIONS]
    return pl.pallas_call(
        _attn_outproj_kernel,
        out_shape=jax.ShapeDtypeStruct(h.shape, F32),
        grid=(bsz, seq // tm),
        in_specs=o_specs + l_specs + [
            pl.BlockSpec(w.shape, lambda b, i: (0, 0)),
            pl.BlockSpec((None, tm, d), tok),
            pl.BlockSpec((None, 6, d), lambda b, i: (b, 0, 0)),
            pl.BlockSpec((1, d), lambda b, i: (0, 0)),
        ],
        out_specs=pl.BlockSpec((None, tm, d), tok),
        scratch_shapes=[pltpu.VMEM((N_HEADS, tm, HEAD_DIM), F32), pltpu.VMEM((N_HEADS, tm, HEAD_DIM), F32),
                        pltpu.VMEM((tm, HEAD_DIM), F32), pltpu.VMEM((tm, HEAD_DIM), F32)],
        compiler_params=_params(("parallel", "parallel"), VMEM_LIMIT_V7X),
        name="attn_out_projection",
    )(*[o for o, _ in outs], *[l for _, l in outs], w, h, mod_l, gain)


FFN_CHUNK = 256


def _ffn_kernel(h_ref, mod_ref, g_in_ref, g_out_ref, w_in_ref, w_out_ref, o_ref):
    u = _normed_input(h_ref, mod_ref, g_in_ref, 3).astype(BF16)
    acc = jnp.zeros(h_ref.shape, F32)
    for c0 in range(0, D_FF, FFN_CHUNK):
        gate = jnp.dot(u, w_in_ref[:, c0:c0 + FFN_CHUNK], preferred_element_type=F32)
        up = jnp.dot(u, w_in_ref[:, D_FF + c0:D_FF + c0 + FFN_CHUNK], preferred_element_type=F32)
        act = (_silu(gate) * up).astype(BF16)
        acc = acc + jnp.dot(act, w_out_ref[c0:c0 + FFN_CHUNK, :], preferred_element_type=F32)
    _residual_update(acc, h_ref, mod_ref, g_out_ref, 5, o_ref)


def _ffn(h, mod_l, g_in, g_out, w_in, w_out, tm=512):
    bsz, seq, d = h.shape
    tok = lambda b, i: (b, i, 0)
    const = lambda b, i: (0, 0)
    return pl.pallas_call(
        _ffn_kernel,
        out_shape=jax.ShapeDtypeStruct(h.shape, F32),
        grid=(bsz, seq // tm),
        in_specs=[
            pl.BlockSpec((None, tm, d), tok),
            pl.BlockSpec((None, 6, d), lambda b, i: (b, 0, 0)),
            pl.BlockSpec((1, d), const),
            pl.BlockSpec((1, d), const),
            pl.BlockSpec(w_in.shape, const, pipeline_mode=pl.Buffered(1)),
            pl.BlockSpec(w_out.shape, const, pipeline_mode=pl.Buffered(1)),
        ],
        out_specs=pl.BlockSpec((None, tm, d), tok),
        compiler_params=_params(("parallel", "parallel"), VMEM_LIMIT_V7X),
        name="swiglu_ffn",
    )(h, mod_l, g_in, g_out, w_in, w_out)


def kernel(x, c, positions, ada_w, ada_b, norm_g, hgrn_w_in, hgrn_lower_bounds, hgrn_norm_g,
           hgrn_w_out, attn_w_in, attn_w_out, ffn_w_in, ffn_w_out):
    depth = ada_w.shape[0]
    bsz, seq, d = x.shape
    mod = _modulation(c, ada_w, ada_b).reshape(depth, bsz, 6, d)
    rope = _rope_tables(positions)
    h = x
    for layer in range(depth):
        mod_l = mod[layer]
        gains = norm_g[layer].reshape(4, 1, d)
        idx = layer // 2
        if layer % 2 == 0:
            h = _hgrn_layer(h, mod_l, gains[0], gains[1], hgrn_lower_bounds,
                            hgrn_norm_g[idx].reshape(1, d), hgrn_w_in[idx].astype(BF16),
                            hgrn_w_out[idx].astype(BF16), idx)
        else:
            w_in = _rope_layout(attn_w_in[idx]).astype(BF16)
            qkv = _attn_in_projection(h, mod_l, gains[0], rope, w_in)
            outs = [_dilated_attention(qkv, g, dil) for g, dil in enumerate(DILATIONS)]
            h = _attn_out_projection(outs, attn_w_out[idx].astype(BF16), h, mod_l, gains[1])
        h = _ffn(h, mod_l, gains[2], gains[3], ffn_w_in[layer].astype(BF16),
                 ffn_w_out[layer].astype(BF16))
    return h
```

```python
import functools

import jax
import jax.numpy as jnp
from jax import lax
from jax.experimental import pallas as pl
from jax.experimental.pallas import tpu as pltpu

F32 = jnp.float32
BF16 = jnp.bfloat16

D_MODEL = 1024
HEAD_DIM = 128
N_HEADS = D_MODEL // HEAD_DIM
HG_CHUNK = 32
HG_BLOCK = 128
HG_SUBTILE = 256
ATT_BLOCK = 128
DILATIONS = (1, 4, 16)
ROPE_DIM = HEAD_DIM // 4
ROPE_HALF = ROPE_DIM // 2
ROPE_THETA = 500000.0
D_FF = -(-(8 * D_MODEL) // (3 * 256)) * 256
NORM_EPS = 1e-6
MXU_COLS_V7X = 256
VMEM_LIMIT_V7X = 56 * 1024 * 1024


def _params(sem, vmem=None):
    return pltpu.CompilerParams(dimension_semantics=sem, vmem_limit_bytes=vmem)


def _rms(x):
    return x * lax.rsqrt(jnp.mean(x * x, axis=-1, keepdims=True) + NORM_EPS)


def _silu(x):
    return x * jax.nn.sigmoid(x)


def _normed_input(h_ref, mod_ref, g_ref, shift_row):
    y = _rms(h_ref[...]) * g_ref[...]
    return y * (1.0 + mod_ref[shift_row + 1:shift_row + 2, :]) + mod_ref[shift_row:shift_row + 1, :]


def _residual_update(y, h_ref, mod_ref, g_ref, gate_row, o_ref):
    o_ref[...] = h_ref[...] + (1.0 + mod_ref[gate_row:gate_row + 1, :]) * (_rms(y) * g_ref[...])


def _mod_kernel(c_ref, w_ref, b_ref, o_ref):
    cond = _silu(c_ref[...])
    o_ref[...] = jnp.dot(cond, w_ref[...], preferred_element_type=F32,
                         precision=lax.Precision.HIGHEST) + b_ref[...]


def _modulation(c, ada_w, ada_b):
    depth, d, n = ada_w.shape
    bsz = c.shape[0]
    tn = 1536
    return pl.pallas_call(
        _mod_kernel,
        out_shape=jax.ShapeDtypeStruct((depth, bsz, n), F32),
        grid=(depth, n // tn),
        in_specs=[
            pl.BlockSpec((bsz, d), lambda l, j: (0, 0)),
            pl.BlockSpec((None, d, tn), lambda l, j: (l, 0, j)),
            pl.BlockSpec((None, 1, tn), lambda l, j: (l, 0, j)),
        ],
        out_specs=pl.BlockSpec((None, bsz, tn), lambda l, j: (l, 0, j)),
        compiler_params=_params(("parallel", "parallel")),
        name="adaln_modulation",
    )(c, ada_w, ada_b.reshape(depth, 1, n))


def _rope_kernel(pos_ref, invf_ref, sgn_ref, cos_ref, sin_ref):
    ang = pos_ref[...].astype(F32) * invf_ref[...]
    cos_ref[...] = jnp.cos(ang)
    sin_ref[...] = jnp.sin(ang) * sgn_ref[...]


def _rope_tables(positions):
    bsz, seq = positions.shape
    ts = 1024
    inv_freq = ROPE_THETA ** (-jnp.arange(0, ROPE_DIM, 2, dtype=F32) / ROPE_DIM)
    zeros = jnp.zeros((HEAD_DIM // 2 - ROPE_HALF,), F32)
    invf = jnp.concatenate([inv_freq, zeros, inv_freq, zeros]).reshape(1, HEAD_DIM)
    sgn = jnp.concatenate([-jnp.ones((ROPE_HALF,), F32), zeros,
                           jnp.ones((ROPE_HALF,), F32), zeros]).reshape(1, HEAD_DIM)
    out = jax.ShapeDtypeStruct((bsz, seq, HEAD_DIM), F32)
    return pl.pallas_call(
        _rope_kernel,
        out_shape=(out, out),
        grid=(bsz, seq // ts),
        in_specs=[
            pl.BlockSpec((None, ts, 1), lambda b, i: (b, i, 0)),
            pl.BlockSpec((1, HEAD_DIM), lambda b, i: (0, 0)),
            pl.BlockSpec((1, HEAD_DIM), lambda b, i: (0, 0)),
        ],
        out_specs=(pl.BlockSpec((None, ts, HEAD_DIM), lambda b, i: (b, i, 0)),
                   pl.BlockSpec((None, ts, HEAD_DIM), lambda b, i: (b, i, 0))),
        compiler_params=_params(("parallel", "parallel")),
        name="rope_tables",
    )(positions.reshape(bsz, seq, 1), invf, sgn)


def _rope_layout(w):
    d = w.shape[0]
    w = w.reshape(d, len(DILATIONS), 3, N_HEADS, HEAD_DIM)
    qk = w[:, :, :2]
    qk = jnp.concatenate([qk[..., :ROPE_HALF], qk[..., ROPE_DIM:HEAD_DIM // 2 + ROPE_HALF],
                          qk[..., ROPE_HALF:ROPE_DIM], qk[..., HEAD_DIM // 2 + ROPE_HALF:]], axis=-1)
    return jnp.concatenate([qk, w[:, :, 2:]], axis=2).reshape(d, -1)


def _scale_chunks(x, vals):
    parts = []
    for j, v in enumerate(vals):
        piece = x[j * HG_CHUNK:(j + 1) * HG_CHUNK, :]
        parts.append(piece if v is None else piece * v)
    return jnp.concatenate(parts, axis=0)


def _hgrn_block(proj_scr, mix_scr, rows, lb, ng_ref, state, masks):
    d = D_MODEL
    diag_mask, off_mask, tril3 = masks
    sig = jax.nn.sigmoid(proj_scr[rows, d:2 * d])
    log_f = jnp.log(lb + (1.0 - lb) * sig)
    k = (1.0 - lb) * (1.0 - sig)
    q = _silu(proj_scr[rows, 0:d])
    hi = log_f.astype(BF16)
    rem = log_f - hi.astype(F32)
    mid = rem.astype(BF16)
    lo = (rem - mid.astype(F32)).astype(BF16)
    b = jnp.dot(tril3, jnp.concatenate([hi, mid, lo], axis=0), preferred_element_type=F32)
    nchunk = HG_BLOCK // HG_CHUNK
    bl = [b[(j + 1) * HG_CHUNK - 1:(j + 1) * HG_CHUNK, :] for j in range(nchunk)]
    b_tot = bl[0] + bl[1] + bl[2] + bl[3]
    q_dec = q * jnp.exp(b)
    k_inv = k * jnp.exp(-b)
    q_mid = _scale_chunks(q_dec, [None, jnp.exp(-bl[1]), None, jnp.exp(bl[2])]).astype(BF16)
    k_mid = _scale_chunks(k_inv, [jnp.exp(bl[0] + bl[1]), jnp.exp(bl[1]), None, None]).astype(BF16)
    q_blk = _scale_chunks(q_dec, [None, jnp.exp(bl[0]), jnp.exp(bl[0] + bl[1]),
                                  jnp.exp(bl[0] + bl[1] + bl[2])]).astype(BF16)
    k_blk = _scale_chunks(k_inv, [jnp.exp(b_tot), jnp.exp(b_tot - bl[0]), jnp.exp(bl[2] + bl[3]),
                                  jnp.exp(bl[3])]).astype(BF16)
    q_dec = q_dec.astype(BF16)
    k_inv = k_inv.astype(BF16)
    dec = jnp.exp(b_tot)

    nt = (((1,), (1,)), ((), ()))
    tn = (((0,), (0,)), ((), ()))
    for hd in range(N_HEADS):
        sl = slice(hd * HEAD_DIM, (hd + 1) * HEAD_DIM)
        v = proj_scr[rows, 2 * d + hd * HEAD_DIM:2 * d + (hd + 1) * HEAD_DIM].astype(BF16)
        gate = proj_scr[rows, 3 * d + hd * HEAD_DIM:3 * d + (hd + 1) * HEAD_DIM]
        a_diag = lax.dot_general(q_dec[:, sl], k_inv[:, sl], nt, preferred_element_type=F32)
        a_off = lax.dot_general(q_mid[:, sl], k_mid[:, sl], nt, preferred_element_type=F32)
        a = jnp.where(diag_mask, a_diag, jnp.where(off_mask, a_off, 0.0)).astype(BF16)
        s0 = state[hd]
        o = jnp.dot(jnp.concatenate([a, q_blk[:, sl]], axis=1),
                    jnp.concatenate([v, s0.astype(BF16)], axis=0), preferred_element_type=F32)
        dec_col = jnp.transpose(jnp.broadcast_to(dec[:, sl], (HEAD_DIM, HEAD_DIM)))
        state[hd] = dec_col * s0 + lax.dot_general(k_blk[:, sl], v, tn, preferred_element_type=F32)
        mix_scr[rows, sl] = (_rms(o) * ng_ref[:, sl] * _silu(gate)).astype(BF16)


def _hgrn_layer_kernel(h_ref, mod_ref, g_in_ref, g_out_ref, lb_ref, ng_ref, w_in_ref, w_out_ref,
                       o_ref, proj_scr, mix_scr, state, *, layer_idx):
    @pl.when(pl.program_id(1) == 0)
    def _():
        state[...] = jnp.zeros_like(state)

    def in_projection(j):
        rows = slice(j * HG_SUBTILE, (j + 1) * HG_SUBTILE)
        y = _rms(h_ref[rows, :]) * g_in_ref[...]
        u = (y * (1.0 + mod_ref[1:2, :]) + mod_ref[0:1, :]).astype(BF16)
        proj_scr[rows, :] = jnp.dot(u, w_in_ref[...], preferred_element_type=F32)

    lbr = lb_ref[...]
    e = jnp.exp(lbr - jnp.max(lbr, axis=0, keepdims=True))
    sm = e / jnp.sum(e, axis=0, keepdims=True)
    csum = sm[0:1, :]
    first = csum
    for r in range(1, layer_idx + 1):
        csum = csum + sm[r:r + 1, :]
    lb = csum - first

    ti = lax.broadcasted_iota(jnp.int32, (HG_BLOCK, HG_BLOCK), 0)
    si = lax.broadcasted_iota(jnp.int32, (HG_BLOCK, HG_BLOCK), 1)
    diag_mask = ((ti // HG_CHUNK) == (si // HG_CHUNK)) & (si <= ti)
    off_mask = (si // HG_CHUNK) < (ti // HG_CHUNK)
    tril = jnp.where(diag_mask, 1.0, 0.0).astype(BF16)
    masks = (diag_mask, off_mask, jnp.concatenate([tril, tril, tril], axis=1))

    n_sub = h_ref.shape[0] // HG_SUBTILE
    in_projection(0)
    for j in range(n_sub):
        if j + 1 < n_sub:
            in_projection(j + 1)
        for blk in range(j * HG_SUBTILE // HG_BLOCK, (j + 1) * HG_SUBTILE // HG_BLOCK):
            rows = slice(blk * HG_BLOCK, (blk + 1) * HG_BLOCK)
            _hgrn_block(proj_scr, mix_scr, rows, lb, ng_ref, state, masks)
    y = jnp.dot(mix_scr[...], w_out_ref[...], preferred_element_type=F32)
    _residual_update(y, h_ref, mod_ref, g_out_ref, 2, o_ref)


def _hgrn_layer(h, mod_l, g_in, g_out, lower_bounds, norm_g, w_in, w_out, layer_idx, tm=512):
    bsz, seq, d = h.shape
    tok = lambda b, i: (b, i, 0)
    const = lambda b, i: (0, 0)
    return pl.pallas_call(
        functools.partial(_hgrn_layer_kernel, layer_idx=layer_idx),
        out_shape=jax.ShapeDtypeStruct(h.shape, F32),
        grid=(bsz, seq // tm),
        in_specs=[
            pl.BlockSpec((None, tm, d), tok),
            pl.BlockSpec((None, 6, d), lambda b, i: (b, 0, 0)),
            pl.BlockSpec((1, d), const),
            pl.BlockSpec((1, d), const),
            pl.BlockSpec(lower_bounds.shape, const),
            pl.BlockSpec((1, d), const),
            pl.BlockSpec(w_in.shape, const, pipeline_mode=pl.Buffered(1)),
            pl.BlockSpec(w_out.shape, const, pipeline_mode=pl.Buffered(1)),
        ],
        out_specs=pl.BlockSpec((None, tm, d), tok),
        scratch_shapes=[pltpu.VMEM((tm, 4 * d), F32), pltpu.VMEM((tm, d), BF16),
                        pltpu.VMEM((N_HEADS, HEAD_DIM, HEAD_DIM), F32)],
        compiler_params=_params(("parallel", "arbitrary"), VMEM_LIMIT_V7X),
        name="hgrn2_layer",
    )(h, mod_l, g_in, g_out, lower_bounds, norm_g, w_in, w_out)


def _residue_rows(ref, dilation):
    if dilation == 1:
        return ref[...]
    n = ref.shape[0] // dilation
    return jnp.concatenate([ref[pl.ds(r, n, stride=dilation), :] for r in range(dilation)], axis=0)


def _attn_inproj_kernel(h_ref, mod_ref, g_ref, cos_ref, sin_ref, w_ref, o_ref,
                        u_slab, u_perm, cos_k, sin_k, cos_q, sin_q):
    group = pl.program_id(2)
    n_slabs = D_MODEL // HEAD_DIM
    scale = HEAD_DIM ** -0.5

    def set_tables(cos, sin):
        cos_k[...] = cos
        sin_k[...] = sin
        cos_q[...] = cos * scale
        sin_q[...] = sin * scale

    @pl.when(group == 0)
    def _():
        u = _normed_input(h_ref, mod_ref, g_ref, 0)
        u_perm[...] = u.astype(BF16)
        for c in range(n_slabs):
            u_slab[c] = u[:, c * HEAD_DIM:(c + 1) * HEAD_DIM]
        set_tables(cos_ref[...], sin_ref[...])

    for g, dilation in enumerate(DILATIONS):
        if dilation == 1:
            continue

        @pl.when(group == g)
        def _():
            for c in range(n_slabs):
                u_perm[:, c * HEAD_DIM:(c + 1) * HEAD_DIM] = (
                    _residue_rows(u_slab.at[c], dilation).astype(BF16))
            set_tables(_residue_rows(cos_ref, dilation), _residue_rows(sin_ref, dilation))

    for c0 in range(0, 3 * D_MODEL, MXU_COLS_V7X):
        acc = jnp.dot(u_perm[...], w_ref[:, c0:c0 + MXU_COLS_V7X], preferred_element_type=F32)
        kind = c0 // D_MODEL
        for c1 in range(0, MXU_COLS_V7X, HEAD_DIM):
            t = acc[:, c1:c1 + HEAD_DIM]
            if kind < 2:
                cos, sin = (cos_q, sin_q) if kind == 0 else (cos_k, sin_k)
                t = t * cos[...] + pltpu.roll(t, HEAD_DIM // 2, axis=1) * sin[...]
            o_ref[:, c0 + c1:c0 + c1 + HEAD_DIM] = t.astype(o_ref.dtype)


def _attn_in_projection(h, mod_l, gain, rope, w, tm=1024):
    bsz, seq, d = h.shape
    tok = lambda b, i, j: (b, i, 0)
    return pl.pallas_call(
        _attn_inproj_kernel,
        out_shape=jax.ShapeDtypeStruct((len(DILATIONS), bsz, seq // tm, tm, 3 * d), BF16),
        grid=(bsz, seq // tm, len(DILATIONS)),
        in_specs=[
            pl.BlockSpec((None, tm, d), tok),
            pl.BlockSpec((None, 6, d), lambda b, i, j: (b, 0, 0)),
            pl.BlockSpec((1, d), lambda b, i, j: (0, 0)),
            pl.BlockSpec((None, tm, HEAD_DIM), tok),
            pl.BlockSpec((None, tm, HEAD_DIM), tok),
            pl.BlockSpec((d, 3 * d), lambda b, i, j: (0, j)),
        ],
        out_specs=pl.BlockSpec((None, None, None, tm, 3 * d), lambda b, i, j: (j, b, i, 0, 0)),
        scratch_shapes=[pltpu.VMEM((d // HEAD_DIM, tm, HEAD_DIM), F32), pltpu.VMEM((tm, d), BF16),
                        pltpu.VMEM((tm, HEAD_DIM), F32), pltpu.VMEM((tm, HEAD_DIM), F32),
                        pltpu.VMEM((tm, HEAD_DIM), F32), pltpu.VMEM((tm, HEAD_DIM), F32)],
        compiler_params=_params(("parallel", "parallel", "arbitrary"), VMEM_LIMIT_V7X),
        name="attn_in_projection",
    )(h, mod_l, gain, rope[0], rope[1], w)


def _block_rows(ref, blk, sl):
    if len(ref.shape) == 2:
        return ref[blk * ATT_BLOCK:(blk + 1) * ATT_BLOCK, sl]
    n = ref.shape[1]
    if n >= ATT_BLOCK:
        start = blk * ATT_BLOCK
        return ref[start // n, start % n:start % n + ATT_BLOCK, sl]
    per = ATT_BLOCK // n
    return jnp.concatenate([ref[blk * per + t, :, sl] for t in range(per)], axis=0)


def _attn_kernel(q_ref, kp_ref, kc_ref, vp_ref, vc_ref, o_ref, lse_ref, *, tq):
    first_tile = pl.program_id(2) == 0
    qi = lax.broadcasted_iota(jnp.int32, (ATT_BLOCK, 2 * ATT_BLOCK), 0)
    ki = lax.broadcasted_iota(jnp.int32, (ATT_BLOCK, 2 * ATT_BLOCK), 1)
    band = (ki >= qi) & (ki <= qi + ATT_BLOCK)
    lane = lax.broadcasted_iota(jnp.int32, (ATT_BLOCK, HEAD_DIM), 1)
    nt = (((1,), (1,)), ((), ()))
    for n in range(tq // ATT_BLOCK):
        rows = slice(n * ATT_BLOCK, (n + 1) * ATT_BLOCK)
        if n == 0:
            k_prev, v_prev, prev_blk = kp_ref, vp_ref, 0
            valid = band & (jnp.logical_not(first_tile) | (ki >= ATT_BLOCK))
        else:
            k_prev, v_prev, prev_blk = kc_ref, vc_ref, n - 1
            valid = band
        lse_tile = jnp.zeros((ATT_BLOCK, HEAD_DIM), F32)
        for hd in range(N_HEADS):
            sl = slice(hd * HEAD_DIM, (hd + 1) * HEAD_DIM)
            kw = jnp.concatenate([_block_rows(k_prev, prev_blk, sl), _block_rows(kc_ref, n, sl)], axis=0)
            vw = jnp.concatenate([_block_rows(v_prev, prev_blk, sl), _block_rows(vc_ref, n, sl)], axis=0)
            s = lax.dot_general(_block_rows(q_ref, n, sl), kw, nt, preferred_element_type=F32)
            s = jnp.where(valid, s, -jnp.inf)
            m = jnp.max(s, axis=-1, keepdims=True)
            p = jnp.exp(s - m)
            z = jnp.sum(p, axis=-1, keepdims=True)
            o = jnp.dot(p.astype(BF16), vw, preferred_element_type=F32) / z
            o_ref[hd, rows, :] = o
            lse_tile = jnp.where(lane == hd, m + jnp.log(z), lse_tile)
        lse_ref[rows, :] = lse_tile


def _dilated_attention(qkv, group, dilation):
    _, bsz, n_tiles, tm, n3 = qkv.shape
    d = D_MODEL
    n = tm // dilation
    sub = n_tiles * n
    tq = min(sub, 512)
    view = qkv.reshape(len(DILATIONS), bsz, n_tiles, dilation, n, n3)

    def prev_rows(which):
        if n >= ATT_BLOCK:
            def index(b, r, i):
                row0 = jnp.maximum(i * tq - ATT_BLOCK, 0)
                return (group, b, row0 // n, r, (row0 % n) // ATT_BLOCK, which)
            return pl.BlockSpec((None, None, None, None, ATT_BLOCK, d), index)
        per = ATT_BLOCK // n
        return pl.BlockSpec(
            (None, None, per, None, n, d),
            lambda b, r, i: (group, b, jnp.maximum(i * (tq // ATT_BLOCK) - 1, 0), r, 0, which))

    def cur_rows(which):
        if n >= tq:
            per = n // tq
            return pl.BlockSpec((None, None, None, None, tq, d),
                                lambda b, r, i: (group, b, i // per, r, i % per, which))
        return pl.BlockSpec((None, None, tq // n, None, n, d),
                            lambda b, r, i: (group, b, i, r, 0, which))

    return pl.pallas_call(
        functools.partial(_attn_kernel, tq=tq),
        out_shape=(jax.ShapeDtypeStruct((bsz, dilation, N_HEADS, sub, HEAD_DIM), F32),
                   jax.ShapeDtypeStruct((bsz, dilation, sub, HEAD_DIM), F32)),
        grid=(bsz, dilation, sub // tq),
        in_specs=[cur_rows(0), prev_rows(1), cur_rows(1), prev_rows(2), cur_rows(2)],
        out_specs=(pl.BlockSpec((None, None, N_HEADS, tq, HEAD_DIM), lambda b, r, i: (b, r, 0, i, 0)),
                   pl.BlockSpec((None, None, tq, HEAD_DIM), lambda b, r, i: (b, r, i, 0))),
        compiler_params=_params(("parallel", "parallel", "parallel"), VMEM_LIMIT_V7X),
        name="dilated_attention",
    )(view, view, view, view, view)


def _token_rows(src_ref, dst_ref):
    dilation, n = src_ref.shape[0], src_ref.shape[-2]
    for r in range(dilation):
        if len(src_ref.shape) == 3:
            dst_ref[pl.ds(r, n, stride=dilation), :] = src_ref[r]
        else:
            for hd in range(src_ref.shape[1]):
                dst_ref[hd, pl.ds(r, n, stride=dilation), :] = src_ref[r, hd]


def _attn_outproj_kernel(o1_ref, o2_ref, o3_ref, l1_ref, l2_ref, l3_ref, w_ref, h_ref, mod_ref,
                         g_ref, o_ref, o2_scr, o3_scr, l2_scr, l3_scr):
    _token_rows(o2_ref, o2_scr)
    _token_rows(o3_ref, o3_scr)
    _token_rows(l2_ref, l2_scr)
    _token_rows(l3_ref, l3_scr)
    l1, l2, l3 = l1_ref[0], l2_scr[...], l3_scr[...]
    m = jnp.maximum(jnp.maximum(l1, l2), l3)
    e1, e2, e3 = jnp.exp(l1 - m), jnp.exp(l2 - m), jnp.exp(l3 - m)
    z = e1 + e2 + e3
    w1, w2, w3 = e1 / z, e2 / z, e3 / z
    pieces = []
    for hd in range(N_HEADS):
        o = (w1[:, hd:hd + 1] * o1_ref[0, hd] + w2[:, hd:hd + 1] * o2_scr[hd]
             + w3[:, hd:hd + 1] * o3_scr[hd])
        pieces.append(o.astype(BF16))
    x = jnp.concatenate(pieces, axis=1)
    y = jnp.dot(x, w_ref[...], preferred_element_type=F32)
    _residual_update(y, h_ref, mod_ref, g_ref, 2, o_ref)


def _attn_out_projection(outs, w, h, mod_l, gain, tm=512):
    bsz, seq, d = h.shape
    tok = lambda b, i: (b, i, 0)
    o_specs = [pl.BlockSpec((None, dil, N_HEADS, tm // dil, HEAD_DIM), lambda b, i: (b, 0, 0, i, 0))
               for dil in DILATIONS]
    l_specs = [pl.BlockSpec((None, dil, tm // dil, HEAD_DIM), lambda b, i: (b, 0, i, 0))
               for dil in DILATIONS]
    return pl.pallas_call(
        _attn_outproj_kernel,
        out_shape=jax.ShapeDtypeStruct(h.shape, F32),
        grid=(bsz, seq // tm),
        in_specs=o_specs + l_specs + [
            pl.BlockSpec(w.shape, lambda b, i: (0, 0)),
            pl.BlockSpec((None, tm, d), tok),
            pl.BlockSpec((None, 6, d), lambda b, i: (b, 0, 0)),
            pl.BlockSpec((1, d), lambda b, i: (0, 0)),
        ],
        out_specs=pl.BlockSpec((None, tm, d), tok),
        scratch_shapes=[pltpu.VMEM((N_HEADS, tm, HEAD_DIM), F32), pltpu.VMEM((N_HEADS, tm, HEAD_DIM), F32),
                        pltpu.VMEM((tm, HEAD_DIM), F32), pltpu.VMEM((tm, HEAD_DIM), F32)],
        compiler_params=_params(("parallel", "parallel"), VMEM_LIMIT_V7X),
        name="attn_out_projection",
    )(*[o for o, _ in outs], *[l for _, l in outs], w, h, mod_l, gain)


FFN_CHUNK = 256


def _ffn_kernel(h_ref, mod_ref, g_in_ref, g_out_ref, w_in_ref, w_out_ref, o_ref):
    u = _normed_input(h_ref, mod_ref, g_in_ref, 3).astype(BF16)
    acc = jnp.zeros(h_ref.shape, F32)
    for c0 in range(0, D_FF, FFN_CHUNK):
        gate = jnp.dot(u, w_in_ref[:, c0:c0 + FFN_CHUNK], preferred_element_type=F32)
        up = jnp.dot(u, w_in_ref[:, D_FF + c0:D_FF + c0 + FFN_CHUNK], preferred_element_type=F32)
        act = (_silu(gate) * up).astype(BF16)
        acc = acc + jnp.dot(act, w_out_ref[c0:c0 + FFN_CHUNK, :], preferred_element_type=F32)
    _residual_update(acc, h_ref, mod_ref, g_out_ref, 5, o_ref)


def _ffn(h, mod_l, g_in, g_out, w_in, w_out, tm=512):
    bsz, seq, d = h.shape
    tok = lambda b, i: (b, i, 0)
    const = lambda b, i: (0, 0)
    return pl.pallas_call(
        _ffn_kernel,
        out_shape=jax.ShapeDtypeStruct(h.shape, F32),
        grid=(bsz, seq // tm),
        in_specs=[
            pl.BlockSpec((None, tm, d), tok),
            pl.BlockSpec((None, 6, d), lambda b, i: (b, 0, 0)),
            pl.BlockSpec((1, d), const),
            pl.BlockSpec((1, d), const),
            pl.BlockSpec(w_in.shape, const, pipeline_mode=pl.Buffered(1)),
            pl.BlockSpec(w_out.shape, const, pipeline_mode=pl.Buffered(1)),
        ],
        out_specs=pl.BlockSpec((None, tm, d), tok),
        compiler_params=_params(("parallel", "parallel"), VMEM_LIMIT_V7X),
        name="swiglu_ffn",
    )(h, mod_l, g_in, g_out, w_in, w_out)


def kernel(x, c, positions, ada_w, ada_b, norm_g, hgrn_w_in, hgrn_lower_bounds, hgrn_norm_g,
           hgrn_w_out, attn_w_in, attn_w_out, ffn_w_in, ffn_w_out):
    depth = ada_w.shape[0]
    bsz, seq, d = x.shape
    mod = _modulation(c, ada_w, ada_b).reshape(depth, bsz, 6, d)
    rope = _rope_tables(positions)
    h = x
    for layer in range(depth):
        mod_l = mod[layer]
        gains = norm_g[layer].reshape(4, 1, d)
        idx = layer // 2
        if layer % 2 == 0:
            h = _hgrn_layer(h, mod_l, gains[0], gains[1], hgrn_lower_bounds,
                            hgrn_norm_g[idx].reshape(1, d), hgrn_w_in[idx].astype(BF16),
                            hgrn_w_out[idx].astype(BF16), idx)
        else:
            w_in = _rope_layout(attn_w_in[idx]).astype(BF16)
            qkv = _attn_in_projection(h, mod_l, gains[0], rope, w_in)
            outs = [_dilated_attention(qkv, g, dil) for g, dil in enumerate(DILATIONS)]
            h = _attn_out_projection(outs, attn_w_out[idx].astype(BF16), h, mod_l, gains[1])
        h = _ffn(h, mod_l, gains[2], gains[3], ffn_w_in[layer].astype(BF16),
                 ffn_w_out[layer].astype(BF16))
    return h
```

```python
import functools

import jax
import jax.numpy as jnp
from jax import lax
from jax.experimental import pallas as pl
from jax.experimental.pallas import tpu as pltpu

F32 = jnp.float32
BF16 = jnp.bfloat16

D_MODEL = 1024
HEAD_DIM = 128
N_HEADS = D_MODEL // HEAD_DIM
HG_CHUNK = 32
HG_BLOCK = 128
HG_SUBTILE = 256
HG_PRE_PIECES = 3
ATT_BLOCK = 128
DILATIONS = (1, 4, 16)
ROPE_DIM = HEAD_DIM // 4
ROPE_HALF = ROPE_DIM // 2
ROPE_THETA = 500000.0
D_FF = -(-(8 * D_MODEL) // (3 * 256)) * 256
NORM_EPS = 1e-6
LOG2_E = 1.4426950408889634
MXU_COLS_V7X = 256
VMEM_LIMIT_V7X = 56 * 1024 * 1024


def _params(sem, vmem=None):
    return pltpu.CompilerParams(dimension_semantics=sem, vmem_limit_bytes=vmem)


def _rms(x):
    return x * lax.rsqrt(jnp.mean(x * x, axis=-1, keepdims=True) + NORM_EPS)


def _silu(x):
    return x * jax.nn.sigmoid(x)


def _silu_tanh(x):
    half = 0.5 * x
    return half + half * jnp.tanh(half)


def _normed_input(h_ref, mod_ref, g_ref, shift_row):
    y = _rms(h_ref[...]) * g_ref[...]
    return y * (1.0 + mod_ref[shift_row + 1:shift_row + 2, :]) + mod_ref[shift_row:shift_row + 1, :]


def _residual_update(y, h_ref, mod_ref, g_ref, gate_row, o_ref):
    o_ref[...] = h_ref[...] + (1.0 + mod_ref[gate_row:gate_row + 1, :]) * (_rms(y) * g_ref[...])


def _mod_kernel(c_ref, w_ref, b_ref, o_ref):
    cond = _silu(c_ref[...])
    o_ref[...] = jnp.dot(cond, w_ref[...], preferred_element_type=F32,
                         precision=lax.Precision.HIGHEST) + b_ref[...]


def _modulation(c, ada_w, ada_b):
    depth, d, n = ada_w.shape
    bsz = c.shape[0]
    tn = 1536
    return pl.pallas_call(
        _mod_kernel,
        out_shape=jax.ShapeDtypeStruct((depth, bsz, n), F32),
        grid=(depth, n // tn),
        in_specs=[
            pl.BlockSpec((bsz, d), lambda l, j: (0, 0)),
            pl.BlockSpec((None, d, tn), lambda l, j: (l, 0, j)),
            pl.BlockSpec((None, 1, tn), lambda l, j: (l, 0, j)),
        ],
        out_specs=pl.BlockSpec((None, bsz, tn), lambda l, j: (l, 0, j)),
        compiler_params=_params(("parallel", "parallel")),
        name="adaln_modulation",
    )(c, ada_w, ada_b.reshape(depth, 1, n))


def _rope_kernel(pos_ref, invf_ref, sgn_ref, cos_ref, sin_ref):
    ang = pos_ref[...].astype(F32) * invf_ref[...]
    cos_ref[...] = jnp.cos(ang)
    sin_ref[...] = jnp.sin(ang) * sgn_ref[...]


def _rope_tables(positions):
    bsz, seq = positions.shape
    ts = 1024
    inv_freq = ROPE_THETA ** (-jnp.arange(0, ROPE_DIM, 2, dtype=F32) / ROPE_DIM)
    zeros = jnp.zeros((HEAD_DIM // 2 - ROPE_HALF,), F32)
    invf = jnp.concatenate([inv_freq, zeros, inv_freq, zeros]).reshape(1, HEAD_DIM)
    sgn = jnp.concatenate([-jnp.ones((ROPE_HALF,), F32), zeros,
                           jnp.ones((ROPE_HALF,), F32), zeros]).reshape(1, HEAD_DIM)
    out = jax.ShapeDtypeStruct((bsz, seq, HEAD_DIM), F32)
    return pl.pallas_call(
        _rope_kernel,
        out_shape=(out, out),
        grid=(bsz, seq // ts),
        in_specs=[
            pl.BlockSpec((None, ts, 1), lambda b, i: (b, i, 0)),
            pl.BlockSpec((1, HEAD_DIM), lambda b, i: (0, 0)),
            pl.BlockSpec((1, HEAD_DIM), lambda b, i: (0, 0)),
        ],
        out_specs=(pl.BlockSpec((None, ts, HEAD_DIM), lambda b, i: (b, i, 0)),
                   pl.BlockSpec((None, ts, HEAD_DIM), lambda b, i: (b, i, 0))),
        compiler_params=_params(("parallel", "parallel")),
        name="rope_tables",
    )(positions.reshape(bsz, seq, 1), invf, sgn)


def _rope_layout(w):
    d = w.shape[0]
    w = w.reshape(d, len(DILATIONS), 3, N_HEADS, HEAD_DIM)
    qk = w[:, :, :2]
    qk = jnp.concatenate([qk[..., :ROPE_HALF], qk[..., ROPE_DIM:HEAD_DIM // 2 + ROPE_HALF],
                          qk[..., ROPE_HALF:ROPE_DIM], qk[..., HEAD_DIM // 2 + ROPE_HALF:]], axis=-1)
    return jnp.concatenate([qk, w[:, :, 2:]], axis=2).reshape(d, -1)


def _scale_chunks(x, vals):
    parts = []
    for j, v in enumerate(vals):
        piece = x[j * HG_CHUNK:(j + 1) * HG_CHUNK, :]
        parts.append(piece if v is None else piece * v)
    return jnp.concatenate(parts, axis=0)


def _hgrn_block(proj_scr, mix_scr, rows, out_rows, lb, ng_ref, state, masks, interleaved):
    d = D_MODEL
    diag_mask, off_mask, tril2 = masks
    for run in interleaved[0]:
        run()
    sig = jax.nn.sigmoid(proj_scr[rows, d:2 * d])
    log_f = jnp.log(lb + (1.0 - lb) * sig) * LOG2_E
    k = (1.0 - lb) * (1.0 - sig)
    q = _silu_tanh(proj_scr[rows, 0:d])
    hi = log_f.astype(BF16)
    lo = (log_f - hi.astype(F32)).astype(BF16)
    b = jnp.dot(tril2, jnp.concatenate([hi, lo], axis=0), preferred_element_type=F32)
    nchunk = HG_BLOCK // HG_CHUNK
    bl = [b[(j + 1) * HG_CHUNK - 1:(j + 1) * HG_CHUNK, :] for j in range(nchunk)]
    b_tot = bl[0] + bl[1] + bl[2] + bl[3]
    q_dec = q * jnp.exp2(b)
    k_inv = k * jnp.exp2(-b)
    q_mid = _scale_chunks(q_dec, [None, jnp.exp2(-bl[1]), None, jnp.exp2(bl[2])]).astype(BF16)
    k_mid = _scale_chunks(k_inv, [jnp.exp2(bl[0] + bl[1]), jnp.exp2(bl[1]), None, None]).astype(BF16)
    q_blk = _scale_chunks(q_dec, [None, jnp.exp2(bl[0]), jnp.exp2(bl[0] + bl[1]),
                                  jnp.exp2(bl[0] + bl[1] + bl[2])]).astype(BF16)
    k_blk = _scale_chunks(k_inv, [jnp.exp2(b_tot), jnp.exp2(b_tot - bl[0]), jnp.exp2(bl[2] + bl[3]),
                                  jnp.exp2(bl[3])]).astype(BF16)
    q_dec = q_dec.astype(BF16)
    k_inv = k_inv.astype(BF16)
    dec = jnp.exp2(b_tot)

    nt = (((1,), (1,)), ((), ()))
    tn = (((0,), (0,)), ((), ()))
    for hd in range(N_HEADS):
        sl = slice(hd * HEAD_DIM, (hd + 1) * HEAD_DIM)
        v = proj_scr[rows, 2 * d + hd * HEAD_DIM:2 * d + (hd + 1) * HEAD_DIM].astype(BF16)
        gate = proj_scr[rows, 3 * d + hd * HEAD_DIM:3 * d + (hd + 1) * HEAD_DIM]
        a_diag = lax.dot_general(q_dec[:, sl], k_inv[:, sl], nt, preferred_element_type=F32)
        a_off = lax.dot_general(q_mid[:, sl], k_mid[:, sl], nt, preferred_element_type=F32)
        a = jnp.where(diag_mask, a_diag, jnp.where(off_mask, a_off, 0.0)).astype(BF16)
        s0 = state[hd]
        o = jnp.dot(jnp.concatenate([a, q_blk[:, sl]], axis=1),
                    jnp.concatenate([v, s0.astype(BF16)], axis=0), preferred_element_type=F32)
        dec_col = jnp.transpose(jnp.broadcast_to(dec[:, sl], (HEAD_DIM, HEAD_DIM)))
        state[hd] = dec_col * s0 + lax.dot_general(k_blk[:, sl], v, tn, preferred_element_type=F32)
        mix_scr[out_rows, sl] = (_rms(o) * ng_ref[:, sl] * _silu_tanh(gate)).astype(BF16)
        for run in interleaved[1 + hd]:
            run()


def _hgrn_layer_kernel(h_ref, hn_ref, mod_ref, modn_ref, g_in_ref, g_out_ref, lb_ref, ng_ref,
                       w_in_ref, w_out_ref, o_ref, *scratch, layer_idx):
    n_sub = h_ref.shape[0] // HG_SUBTILE
    proj_scrs, u_scrs, (mix_scr, state) = scratch[:n_sub], scratch[n_sub:2 * n_sub], scratch[-2:]
    blocks_per_sub = HG_SUBTILE // HG_BLOCK
    piece = MXU_COLS_V7X
    n_pieces = 4 * D_MODEL // piece

    @pl.when(pl.program_id(1) == 0)
    def _():
        state[...] = jnp.zeros_like(state)

    def in_projection(j, x, shift_scale_ref):
        y = _rms(x) * g_in_ref[...]
        u_scrs[j][...] = (y * (1.0 + shift_scale_ref[1:2, :]) + shift_scale_ref[0:1, :]).astype(BF16)

        def make(c):
            def run():
                cols = slice(c * piece, (c + 1) * piece)
                proj_scrs[j][:, cols] = jnp.dot(u_scrs[j][...], w_in_ref[:, cols],
                                                preferred_element_type=F32)
            return run
        per_block = n_pieces // blocks_per_sub
        slots = []
        for blk in range(blocks_per_sub):
            mine = [make(blk * per_block + c) for c in range(per_block)]
            block_slots = [[] for _ in range(N_HEADS + 1)]
            block_slots[0] = mine[:HG_PRE_PIECES]
            rest = mine[HG_PRE_PIECES:]
            for c, run in enumerate(rest):
                block_slots[1 + c * N_HEADS // len(rest)].append(run)
            slots.append(block_slots)
        return slots

    lbr = lb_ref[...]
    e = jnp.exp(lbr - jnp.max(lbr, axis=0, keepdims=True))
    sm = e / jnp.sum(e, axis=0, keepdims=True)
    csum = sm[0:1, :]
    first = csum
    for r in range(1, layer_idx + 1):
        csum = csum + sm[r:r + 1, :]
    lb = csum - first

    ti = lax.broadcasted_iota(jnp.int32, (HG_BLOCK, HG_BLOCK), 0)
    si = lax.broadcasted_iota(jnp.int32, (HG_BLOCK, HG_BLOCK), 1)
    diag_mask = ((ti // HG_CHUNK) == (si // HG_CHUNK)) & (si <= ti)
    off_mask = (si // HG_CHUNK) < (ti // HG_CHUNK)
    tril = jnp.where(diag_mask, 1.0, 0.0).astype(BF16)
    masks = (diag_mask, off_mask, jnp.concatenate([tril, tril], axis=1))

    @pl.when((pl.program_id(0) == 0) & (pl.program_id(1) == 0))
    def _():
        for block_slots in in_projection(0, h_ref[0:HG_SUBTILE, :], mod_ref):
            for slot in block_slots:
                for run in slot:
                    run()

    for j in range(n_sub):
        if j + 1 < n_sub:
            pieces = in_projection(j + 1, h_ref[(j + 1) * HG_SUBTILE:(j + 2) * HG_SUBTILE, :], mod_ref)
        else:
            pieces = in_projection(0, hn_ref[...], modn_ref)
        for blk in range(blocks_per_sub):
            rows = slice(blk * HG_BLOCK, (blk + 1) * HG_BLOCK)
            out_rows = slice(j * HG_SUBTILE + blk * HG_BLOCK, j * HG_SUBTILE + (blk + 1) * HG_BLOCK)
            _hgrn_block(proj_scrs[j], mix_scr, rows, out_rows, lb, ng_ref, state, masks, pieces[blk])
    y = jnp.dot(mix_scr[...], w_out_ref[...], preferred_element_type=F32)
    _residual_update(y, h_ref, mod_ref, g_out_ref, 2, o_ref)


def _hgrn_layer(h, mod_l, g_in, g_out, lower_bounds, norm_g, w_in, w_out, layer_idx, tm=512):
    bsz, seq, d = h.shape
    tok = lambda b, i: (b, i, 0)
    const = lambda b, i: (0, 0)
    n_tiles = seq // tm

    def next_tile(b, i):
        t = jnp.minimum(b * n_tiles + i + 1, bsz * n_tiles - 1)
        return t // n_tiles, t % n_tiles

    def next_rows(b, i):
        nb, ni = next_tile(b, i)
        return nb, ni * (tm // HG_SUBTILE), 0

    return pl.pallas_call(
        functools.partial(_hgrn_layer_kernel, layer_idx=layer_idx),
        out_shape=jax.ShapeDtypeStruct(h.shape, F32),
        grid=(bsz, n_tiles),
        in_specs=[
            pl.BlockSpec((None, tm, d), tok),
            pl.BlockSpec((None, HG_SUBTILE, d), next_rows),
            pl.BlockSpec((None, 6, d), lambda b, i: (b, 0, 0)),
            pl.BlockSpec((None, 6, d), lambda b, i: (next_tile(b, i)[0], 0, 0)),
            pl.BlockSpec((1, d), const),
            pl.BlockSpec((1, d), const),
            pl.BlockSpec(lower_bounds.shape, const),
            pl.BlockSpec((1, d), const),
            pl.BlockSpec(w_in.shape, const, pipeline_mode=pl.Buffered(1)),
            pl.BlockSpec(w_out.shape, const, pipeline_mode=pl.Buffered(1)),
        ],
        out_specs=pl.BlockSpec((None, tm, d), tok),
        scratch_shapes=[pltpu.VMEM((HG_SUBTILE, 4 * d), F32)] * (tm // HG_SUBTILE)
        + [pltpu.VMEM((HG_SUBTILE, d), BF16)] * (tm // HG_SUBTILE)
        + [pltpu.VMEM((tm, d), BF16), pltpu.VMEM((N_HEADS, HEAD_DIM, HEAD_DIM), F32)],
        compiler_params=_params(("arbitrary", "arbitrary"), VMEM_LIMIT_V7X),
        name="hgrn2_layer",
    )(h, h, mod_l, mod_l, g_in, g_out, lower_bounds, norm_g, w_in, w_out)


def _residue_rows(ref, dilation):
    if dilation == 1:
        return ref[...]
    n = ref.shape[0] // dilation
    return jnp.concatenate([ref[pl.ds(r, n, stride=dilation), :] for r in range(dilation)], axis=0)


def _attn_inproj_kernel(h_ref, mod_ref, g_ref, cos_ref, sin_ref, w_ref, o_ref,
                        u_slab, u_perm, cos_k, sin_k, cos_q, sin_q):
    group = pl.program_id(2)
    n_slabs = D_MODEL // HEAD_DIM
    scale = HEAD_DIM ** -0.5

    def set_tables(cos, sin):
        cos_k[...] = cos
        sin_k[...] = sin
        cos_q[...] = cos * scale
        sin_q[...] = sin * scale

    @pl.when(group == 0)
    def _():
        u = _normed_input(h_ref, mod_ref, g_ref, 0)
        u_perm[...] = u.astype(BF16)
        for c in range(n_slabs):
            u_slab[c] = u[:, c * HEAD_DIM:(c + 1) * HEAD_DIM]
        set_tables(cos_ref[...], sin_ref[...])

    for g, dilation in enumerate(DILATIONS):
        if dilation == 1:
            continue

        @pl.when(group == g)
        def _():
            for c in range(n_slabs):
                u_perm[:, c * HEAD_DIM:(c + 1) * HEAD_DIM] = (
                    _residue_rows(u_slab.at[c], dilation).astype(BF16))
            set_tables(_residue_rows(cos_ref, dilation), _residue_rows(sin_ref, dilation))

    for c0 in range(0, 3 * D_MODEL, MXU_COLS_V7X):
        acc = jnp.dot(u_perm[...], w_ref[:, c0:c0 + MXU_COLS_V7X], preferred_element_type=F32)
        kind = c0 // D_MODEL
        for c1 in range(0, MXU_COLS_V7X, HEAD_DIM):
            t = acc[:, c1:c1 + HEAD_DIM]
            if kind < 2:
                cos, sin = (cos_q, sin_q) if kind == 0 else (cos_k, sin_k)
                t = t * cos[...] + pltpu.roll(t, HEAD_DIM // 2, axis=1) * sin[...]
            o_ref[:, c0 + c1:c0 + c1 + HEAD_DIM] = t.astype(o_ref.dtype)


def _attn_in_projection(h, mod_l, gain, rope, w, tm=1024):
    bsz, seq, d = h.shape
    tok = lambda b, i, j: (b, i, 0)
    return pl.pallas_call(
        _attn_inproj_kernel,
        out_shape=jax.ShapeDtypeStruct((len(DILATIONS), bsz, seq // tm, tm, 3 * d), BF16),
        grid=(bsz, seq // tm, len(DILATIONS)),
        in_specs=[
            pl.BlockSpec((None, tm, d), tok),
            pl.BlockSpec((None, 6, d), lambda b, i, j: (b, 0, 0)),
            pl.BlockSpec((1, d), lambda b, i, j: (0, 0)),
            pl.BlockSpec((None, tm, HEAD_DIM), tok),
            pl.BlockSpec((None, tm, HEAD_DIM), tok),
            pl.BlockSpec((d, 3 * d), lambda b, i, j: (0, j)),
        ],
        out_specs=pl.BlockSpec((None, None, None, tm, 3 * d), lambda b, i, j: (j, b, i, 0, 0)),
        scratch_shapes=[pltpu.VMEM((d // HEAD_DIM, tm, HEAD_DIM), F32), pltpu.VMEM((tm, d), BF16),
                        pltpu.VMEM((tm, HEAD_DIM), F32), pltpu.VMEM((tm, HEAD_DIM), F32),
                        pltpu.VMEM((tm, HEAD_DIM), F32), pltpu.VMEM((tm, HEAD_DIM), F32)],
        compiler_params=_params(("parallel", "parallel", "arbitrary"), VMEM_LIMIT_V7X),
        name="attn_in_projection",
    )(h, mod_l, gain, rope[0], rope[1], w)


def _class_rows(ref, r, blk, sl):
    if len(ref.shape) == 3:
        return ref[r, blk * ATT_BLOCK:(blk + 1) * ATT_BLOCK, sl]
    return jnp.concatenate([ref[t, r, :, sl] for t in range(ref.shape[0])], axis=0)


def _attn_kernel(q_ref, kp_ref, kc_ref, vp_ref, vc_ref, o_ref, lse_ref, *, dilation, n_blocks):
    n_classes = q_ref.shape[-3]
    first_tile = pl.program_id(1) == 0
    class0 = pl.program_id(2) * n_classes
    qi = lax.broadcasted_iota(jnp.int32, (ATT_BLOCK, 2 * ATT_BLOCK), 0)
    ki = lax.broadcasted_iota(jnp.int32, (ATT_BLOCK, 2 * ATT_BLOCK), 1)
    band = (ki >= qi) & (ki <= qi + ATT_BLOCK)
    band_first = band & (jnp.logical_not(first_tile) | (ki >= ATT_BLOCK))
    lane = lax.broadcasted_iota(jnp.int32, (ATT_BLOCK, HEAD_DIM), 1)
    nt = (((1,), (1,)), ((), ()))
    for r in range(n_classes):
        for blk in range(n_blocks):
            if blk == 0:
                k_prev, v_prev, prev_blk, valid = kp_ref, vp_ref, 0, band_first
            else:
                k_prev, v_prev, prev_blk, valid = kc_ref, vc_ref, blk - 1, band
            if dilation == 1:
                rows = slice(blk * ATT_BLOCK, (blk + 1) * ATT_BLOCK)
            else:
                rows = pl.ds(class0 + r + blk * ATT_BLOCK * dilation, ATT_BLOCK, stride=dilation)
            lse_tile = jnp.zeros((ATT_BLOCK, HEAD_DIM), F32)
            for hd in range(N_HEADS):
                sl = slice(hd * HEAD_DIM, (hd + 1) * HEAD_DIM)
                kw = jnp.concatenate([_class_rows(k_prev, r, prev_blk, sl),
                                      _class_rows(kc_ref, r, blk, sl)], axis=0)
                vw = jnp.concatenate([_class_rows(v_prev, r, prev_blk, sl),
                                      _class_rows(vc_ref, r, blk, sl)], axis=0)
                s = lax.dot_general(_class_rows(q_ref, r, blk, sl), kw, nt, preferred_element_type=F32)
                s = jnp.where(valid, s, -jnp.inf)
                m = jnp.max(s, axis=-1, keepdims=True)
                p = jnp.exp(s - m)
                z = jnp.sum(p, axis=-1, keepdims=True)
                o_ref[hd, rows, :] = jnp.dot(p.astype(BF16), vw, preferred_element_type=F32) / z
                lse_tile = jnp.where(lane == hd, m + jnp.log(z), lse_tile)
            lse_ref[rows, :] = lse_tile


def _dilated_attention(qkv, group, dilation):
    _, bsz, n_tiles, tm, n3 = qkv.shape
    d = D_MODEL
    seq = n_tiles * tm
    n = tm // dilation
    view = qkv.reshape(len(DILATIONS), bsz, n_tiles, dilation, n, n3)
    n_blocks = 4 if dilation == 1 else 1
    rows = n_blocks * ATT_BLOCK
    tokens = rows * dilation
    n_classes = min(dilation, 8)

    if n >= rows:
        per = n // rows

        def cur_rows(which):
            return pl.BlockSpec((None, None, None, n_classes, rows, d),
                                lambda b, i, c: (group, b, i // per, c, i % per, which))

        def prev_rows(which):
            def index(b, i, c):
                row0 = jnp.maximum(i * rows - ATT_BLOCK, 0)
                return (group, b, row0 // n, c, (row0 % n) // ATT_BLOCK, which)
            return pl.BlockSpec((None, None, None, n_classes, ATT_BLOCK, d), index)
    else:
        span = rows // n

        def cur_rows(which):
            return pl.BlockSpec((None, None, span, n_classes, n, d),
                                lambda b, i, c: (group, b, i, c, 0, which))

        def prev_rows(which):
            return pl.BlockSpec((None, None, span, n_classes, n, d),
                                lambda b, i, c: (group, b, jnp.maximum(i - 1, 0), c, 0, which))

    return pl.pallas_call(
        functools.partial(_attn_kernel, dilation=dilation, n_blocks=n_blocks),
        out_shape=(jax.ShapeDtypeStruct((bsz, N_HEADS, seq, HEAD_DIM), F32),
                   jax.ShapeDtypeStruct((bsz, seq, HEAD_DIM), F32)),
        grid=(bsz, seq // tokens, dilation // n_classes),
        in_specs=[cur_rows(0), prev_rows(1), cur_rows(1), prev_rows(2), cur_rows(2)],
        out_specs=(pl.BlockSpec((None, N_HEADS, tokens, HEAD_DIM), lambda b, i, c: (b, 0, i, 0)),
                   pl.BlockSpec((None, tokens, HEAD_DIM), lambda b, i, c: (b, i, 0))),
        compiler_params=_params(("parallel", "parallel", "arbitrary"), VMEM_LIMIT_V7X),
        name="dilated_attention",
    )(view, view, view, view, view)


def _attn_outproj_kernel(o1_ref, o2_ref, o3_ref, l1_ref, l2_ref, l3_ref, w_ref, h_ref, mod_ref,
                         g_ref, o_ref):
    l1, l2, l3 = l1_ref[...], l2_ref[...], l3_ref[...]
    m = jnp.maximum(jnp.maximum(l1, l2), l3)
    e1, e2, e3 = jnp.exp(l1 - m), jnp.exp(l2 - m), jnp.exp(l3 - m)
    z = e1 + e2 + e3
    w1, w2 = e1 / z, e2 / z
    pieces = []
    for hd in range(N_HEADS):
        o3 = o3_ref[hd]
        o = o3 + w1[:, hd:hd + 1] * (o1_ref[hd] - o3) + w2[:, hd:hd + 1] * (o2_ref[hd] - o3)
        pieces.append(o.astype(BF16))
    x = jnp.concatenate(pieces, axis=1)
    y = jnp.dot(x, w_ref[...], preferred_element_type=F32)
    _residual_update(y, h_ref, mod_ref, g_ref, 2, o_ref)


def _attn_out_projection(outs, w, h, mod_l, gain, tm=512):
    bsz, seq, d = h.shape
    tok = lambda b, i: (b, i, 0)
    o_spec = pl.BlockSpec((None, N_HEADS, tm, HEAD_DIM), lambda b, i: (b, 0, i, 0))
    l_spec = pl.BlockSpec((None, tm, HEAD_DIM), tok)
    return pl.pallas_call(
        _attn_outproj_kernel,
        out_shape=jax.ShapeDtypeStruct(h.shape, F32),
        grid=(bsz, seq // tm),
        in_specs=[o_spec] * len(outs) + [l_spec] * len(outs) + [
            pl.BlockSpec(w.shape, lambda b, i: (0, 0)),
            pl.BlockSpec((None, tm, d), tok),
            pl.BlockSpec((None, 6, d), lambda b, i: (b, 0, 0)),
            pl.BlockSpec((1, d), lambda b, i: (0, 0)),
        ],
        out_specs=pl.BlockSpec((None, tm, d), tok),
        compiler_params=_params(("parallel", "parallel"), VMEM_LIMIT_V7X),
        name="attn_out_projection",
    )(*[o for o, _ in outs], *[l for _, l in outs], w, h, mod_l, gain)


FFN_CHUNK = 256


def _ffn_kernel(h_ref, mod_ref, g_in_ref, g_out_ref, w_in_ref, w_out_ref, o_ref):
    u = _normed_input(h_ref, mod_ref, g_in_ref, 3).astype(BF16)
    acc = jnp.zeros(h_ref.shape, F32)
    for c0 in range(0, D_FF, FFN_CHUNK):
        gate = jnp.dot(u, w_in_ref[:, c0:c0 + FFN_CHUNK], preferred_element_type=F32)
        up = jnp.dot(u, w_in_ref[:, D_FF + c0:D_FF + c0 + FFN_CHUNK], preferred_element_type=F32)
        act = (_silu(gate) * up).astype(BF16)
        acc = acc + jnp.dot(act, w_out_ref[c0:c0 + FFN_CHUNK, :], preferred_element_type=F32)
    _residual_update(acc, h_ref, mod_ref, g_out_ref, 5, o_ref)


def _ffn(h, mod_l, g_in, g_out, w_in, w_out, tm=512):
    bsz, seq, d = h.shape
    tok = lambda b, i: (b, i, 0)
    const = lambda b, i: (0, 0)
    return pl.pallas_call(
        _ffn_kernel,
        out_shape=jax.ShapeDtypeStruct(h.shape, F32),
        grid=(bsz, seq // tm),
        in_specs=[
            pl.BlockSpec((None, tm, d), tok),
            pl.BlockSpec((None, 6, d), lambda b, i: (b, 0, 0)),
            pl.BlockSpec((1, d), const),
            pl.BlockSpec((1, d), const),
            pl.BlockSpec(w_in.shape, const, pipeline_mode=pl.Buffered(1)),
            pl.BlockSpec(w_out.shape, const, pipeline_mode=pl.Buffered(1)),
        ],
        out_specs=pl.BlockSpec((None, tm, d), tok),
        compiler_params=_params(("parallel", "parallel"), VMEM_LIMIT_V7X),
        name="swiglu_ffn",
    )(h, mod_l, g_in, g_out, w_in, w_out)


def kernel(x, c, positions, ada_w, ada_b, norm_g, hgrn_w_in, hgrn_lower_bounds, hgrn_norm_g,
           hgrn_w_out, attn_w_in, attn_w_out, ffn_w_in, ffn_w_out):
    depth = ada_w.shape[0]
    bsz, seq, d = x.shape
    mod = _modulation(c, ada_w, ada_b).reshape(depth, bsz, 6, d)
    rope = _rope_tables(positions)
    h = x
    for layer in range(depth):
        mod_l = mod[layer]
        gains = norm_g[layer].reshape(4, 1, d)
        idx = layer // 2
        if layer % 2 == 0:
            h = _hgrn_layer(h, mod_l, gains[0], gains[1], hgrn_lower_bounds,
                            hgrn_norm_g[idx].reshape(1, d), hgrn_w_in[idx].astype(BF16),
                            hgrn_w_out[idx].astype(BF16), idx)
        else:
            w_in = _rope_layout(attn_w_in[idx]).astype(BF16)
            qkv = _attn_in_projection(h, mod_l, gains[0], rope, w_in)
            outs = [_dilated_attention(qkv, g, dil) for g, dil in enumerate(DILATIONS)]
            h = _attn_out_projection(outs, attn_w_out[idx].astype(BF16), h, mod_l, gains[1])
        h = _ffn(h, mod_l, gains[2], gains[3], ffn_w_in[layer].astype(BF16),
                 ffn_w_out[layer].astype(BF16))
    return h
```

```python
import functools

import jax
import jax.numpy as jnp
from jax import lax
from jax.experimental import pallas as pl
from jax.experimental.pallas import tpu as pltpu

F32 = jnp.float32
BF16 = jnp.bfloat16

D_MODEL = 1024
HEAD_DIM = 128
N_HEADS = D_MODEL // HEAD_DIM
HG_CHUNK = 32
HG_BLOCK = 128
HG_SUBTILE = 256
HG_PRE_PIECES = 3
ATT_BLOCK = 128
DILATIONS = (1, 4, 16)
ROPE_DIM = HEAD_DIM // 4
ROPE_HALF = ROPE_DIM // 2
ROPE_THETA = 500000.0
D_FF = -(-(8 * D_MODEL) // (3 * 256)) * 256
NORM_EPS = 1e-6
LOG2_E = 1.4426950408889634
MXU_COLS_V7X = 256
VMEM_LIMIT_V7X = 56 * 1024 * 1024


def _params(sem, vmem=None):
    return pltpu.CompilerParams(dimension_semantics=sem, vmem_limit_bytes=vmem)


def _rms(x):
    return x * lax.rsqrt(jnp.mean(x * x, axis=-1, keepdims=True) + NORM_EPS)


def _silu(x):
    return x * jax.nn.sigmoid(x)


def _silu_tanh(x):
    half = 0.5 * x
    return half + half * jnp.tanh(half)


def _normed_input(h_ref, mod_ref, g_ref, shift_row):
    y = _rms(h_ref[...]) * g_ref[...]
    return y * (1.0 + mod_ref[shift_row + 1:shift_row + 2, :]) + mod_ref[shift_row:shift_row + 1, :]


def _residual_update(y, h_ref, mod_ref, g_ref, gate_row, o_ref):
    o_ref[...] = h_ref[...] + (1.0 + mod_ref[gate_row:gate_row + 1, :]) * (_rms(y) * g_ref[...])


def _mod_kernel(c_ref, w_ref, b_ref, o_ref):
    cond = _silu(c_ref[...])
    o_ref[...] = jnp.dot(cond, w_ref[...], preferred_element_type=F32,
                         precision=lax.Precision.HIGHEST) + b_ref[...]


def _modulation(c, ada_w, ada_b):
    depth, d, n = ada_w.shape
    bsz = c.shape[0]
    tn = 1536
    return pl.pallas_call(
        _mod_kernel,
        out_shape=jax.ShapeDtypeStruct((depth, bsz, n), F32),
        grid=(depth, n // tn),
        in_specs=[
            pl.BlockSpec((bsz, d), lambda l, j: (0, 0)),
            pl.BlockSpec((None, d, tn), lambda l, j: (l, 0, j)),
            pl.BlockSpec((None, 1, tn), lambda l, j: (l, 0, j)),
        ],
        out_specs=pl.BlockSpec((None, bsz, tn), lambda l, j: (l, 0, j)),
        compiler_params=_params(("parallel", "parallel")),
        name="adaln_modulation",
    )(c, ada_w, ada_b.reshape(depth, 1, n))


def _rope_kernel(pos_ref, invf_ref, sgn_ref, cos_ref, sin_ref):
    ang = pos_ref[...].astype(F32) * invf_ref[...]
    cos_ref[...] = jnp.cos(ang)
    sin_ref[...] = jnp.sin(ang) * sgn_ref[...]


def _rope_tables(positions):
    bsz, seq = positions.shape
    ts = 1024
    inv_freq = ROPE_THETA ** (-jnp.arange(0, ROPE_DIM, 2, dtype=F32) / ROPE_DIM)
    zeros = jnp.zeros((HEAD_DIM // 2 - ROPE_HALF,), F32)
    invf = jnp.concatenate([inv_freq, zeros, inv_freq, zeros]).reshape(1, HEAD_DIM)
    sgn = jnp.concatenate([-jnp.ones((ROPE_HALF,), F32), zeros,
                           jnp.ones((ROPE_HALF,), F32), zeros]).reshape(1, HEAD_DIM)
    out = jax.ShapeDtypeStruct((bsz, seq, HEAD_DIM), F32)
    return pl.pallas_call(
        _rope_kernel,
        out_shape=(out, out),
        grid=(bsz, seq // ts),
        in_specs=[
            pl.BlockSpec((None, ts, 1), lambda b, i: (b, i, 0)),
            pl.BlockSpec((1, HEAD_DIM), lambda b, i: (0, 0)),
            pl.BlockSpec((1, HEAD_DIM), lambda b, i: (0, 0)),
        ],
        out_specs=(pl.BlockSpec((None, ts, HEAD_DIM), lambda b, i: (b, i, 0)),
                   pl.BlockSpec((None, ts, HEAD_DIM), lambda b, i: (b, i, 0))),
        compiler_params=_params(("parallel", "parallel")),
        name="rope_tables",
    )(positions.reshape(bsz, seq, 1), invf, sgn)


def _rope_layout(w):
    d = w.shape[0]
    w = w.reshape(d, len(DILATIONS), 3, N_HEADS, HEAD_DIM)
    qk = w[:, :, :2]
    qk = jnp.concatenate([qk[..., :ROPE_HALF], qk[..., ROPE_DIM:HEAD_DIM // 2 + ROPE_HALF],
                          qk[..., ROPE_HALF:ROPE_DIM], qk[..., HEAD_DIM // 2 + ROPE_HALF:]], axis=-1)
    return jnp.concatenate([qk, w[:, :, 2:]], axis=2).reshape(d, -1)


def _scale_chunks(x, vals):
    parts = []
    for j, v in enumerate(vals):
        piece = x[j * HG_CHUNK:(j + 1) * HG_CHUNK, :]
        parts.append(piece if v is None else piece * v)
    return jnp.concatenate(parts, axis=0)


def _hgrn_block(proj_scr, mix_scr, rows, out_rows, lb, ng_ref, state, masks, interleaved):
    d = D_MODEL
    diag_mask, off_mask, tril2 = masks
    for run in interleaved[0]:
        run()
    sig = jax.nn.sigmoid(proj_scr[rows, d:2 * d])
    log_f = jnp.log(lb + (1.0 - lb) * sig) * LOG2_E
    k = (1.0 - lb) * (1.0 - sig)
    q = _silu_tanh(proj_scr[rows, 0:d])
    hi = log_f.astype(BF16)
    lo = (log_f - hi.astype(F32)).astype(BF16)
    b = jnp.dot(tril2, jnp.concatenate([hi, lo], axis=0), preferred_element_type=F32)
    nchunk = HG_BLOCK // HG_CHUNK
    bl = [b[(j + 1) * HG_CHUNK - 1:(j + 1) * HG_CHUNK, :] for j in range(nchunk)]
    b_tot = bl[0] + bl[1] + bl[2] + bl[3]
    q_dec = q * jnp.exp2(b)
    k_inv = k * jnp.exp2(-b)
    q_mid = _scale_chunks(q_dec, [None, jnp.exp2(-bl[1]), None, jnp.exp2(bl[2])]).astype(BF16)
    k_mid = _scale_chunks(k_inv, [jnp.exp2(bl[0] + bl[1]), jnp.exp2(bl[1]), None, None]).astype(BF16)
    q_blk = _scale_chunks(q_dec, [None, jnp.exp2(bl[0]), jnp.exp2(bl[0] + bl[1]),
                                  jnp.exp2(bl[0] + bl[1] + bl[2])]).astype(BF16)
    k_blk = _scale_chunks(k_inv, [jnp.exp2(b_tot), jnp.exp2(b_tot - bl[0]), jnp.exp2(bl[2] + bl[3]),
                                  jnp.exp2(bl[3])]).astype(BF16)
    q_dec = q_dec.astype(BF16)
    k_inv = k_inv.astype(BF16)
    dec = jnp.exp2(b_tot)

    nt = (((1,), (1,)), ((), ()))
    tn = (((0,), (0,)), ((), ()))
    for hd in range(N_HEADS):
        sl = slice(hd * HEAD_DIM, (hd + 1) * HEAD_DIM)
        v = proj_scr[rows, 2 * d + hd * HEAD_DIM:2 * d + (hd + 1) * HEAD_DIM].astype(BF16)
        gate = proj_scr[rows, 3 * d + hd * HEAD_DIM:3 * d + (hd + 1) * HEAD_DIM]
        a_diag = lax.dot_general(q_dec[:, sl], k_inv[:, sl], nt, preferred_element_type=F32)
        a_off = lax.dot_general(q_mid[:, sl], k_mid[:, sl], nt, preferred_element_type=F32)
        a = jnp.where(diag_mask, a_diag, jnp.where(off_mask, a_off, 0.0)).astype(BF16)
        s0 = state[hd]
        o = jnp.dot(jnp.concatenate([a, q_blk[:, sl]], axis=1),
                    jnp.concatenate([v, s0.astype(BF16)], axis=0), preferred_element_type=F32)
        dec_col = jnp.transpose(jnp.broadcast_to(dec[:, sl], (HEAD_DIM, HEAD_DIM)))
        state[hd] = dec_col * s0 + lax.dot_general(k_blk[:, sl], v, tn, preferred_element_type=F32)
        mix_scr[out_rows, sl] = (_rms(o) * ng_ref[:, sl] * _silu_tanh(gate)).astype(BF16)
        for run in interleaved[1 + hd]:
            run()


def _hgrn_layer_kernel(h_ref, hn_ref, mod_ref, modn_ref, g_in_ref, g_out_ref, lb_ref, ng_ref,
                       w_in_ref, w_out_ref, o_ref, *scratch, layer_idx):
    n_sub = h_ref.shape[0] // HG_SUBTILE
    proj_scrs, u_scrs, (mix_scr, state) = scratch[:n_sub], scratch[n_sub:2 * n_sub], scratch[-2:]
    blocks_per_sub = HG_SUBTILE // HG_BLOCK
    piece = MXU_COLS_V7X
    n_pieces = 4 * D_MODEL // piece

    @pl.when(pl.program_id(1) == 0)
    def _():
        state[...] = jnp.zeros_like(state)

    def in_projection(j, x, shift_scale_ref):
        y = _rms(x) * g_in_ref[...]
        u_scrs[j][...] = (y * (1.0 + shift_scale_ref[1:2, :]) + shift_scale_ref[0:1, :]).astype(BF16)

        def make(c):
            def run():
                cols = slice(c * piece, (c + 1) * piece)
                proj_scrs[j][:, cols] = jnp.dot(u_scrs[j][...], w_in_ref[:, cols],
                                                preferred_element_type=F32)
            return run
        per_block = n_pieces // blocks_per_sub
        slots = []
        for blk in range(blocks_per_sub):
            mine = [make(blk * per_block + c) for c in range(per_block)]
            block_slots = [[] for _ in range(N_HEADS + 1)]
            block_slots[0] = mine[:HG_PRE_PIECES]
            rest = mine[HG_PRE_PIECES:]
            for c, run in enumerate(rest):
                block_slots[1 + c * N_HEADS // len(rest)].append(run)
            slots.append(block_slots)
        return slots

    lbr = lb_ref[...]
    e = jnp.exp(lbr - jnp.max(lbr, axis=0, keepdims=True))
    sm = e / jnp.sum(e, axis=0, keepdims=True)
    csum = sm[0:1, :]
    first = csum
    for r in range(1, layer_idx + 1):
        csum = csum + sm[r:r + 1, :]
    lb = csum - first

    ti = lax.broadcasted_iota(jnp.int32, (HG_BLOCK, HG_BLOCK), 0)
    si = lax.broadcasted_iota(jnp.int32, (HG_BLOCK, HG_BLOCK), 1)
    diag_mask = ((ti // HG_CHUNK) == (si // HG_CHUNK)) & (si <= ti)
    off_mask = (si // HG_CHUNK) < (ti // HG_CHUNK)
    tril = jnp.where(diag_mask, 1.0, 0.0).astype(BF16)
    masks = (diag_mask, off_mask, jnp.concatenate([tril, tril], axis=1))

    @pl.when((pl.program_id(0) == 0) & (pl.program_id(1) == 0))
    def _():
        for block_slots in in_projection(0, h_ref[0:HG_SUBTILE, :], mod_ref):
            for slot in block_slots:
                for run in slot:
                    run()

    for j in range(n_sub):
        if j + 1 < n_sub:
            pieces = in_projection(j + 1, h_ref[(j + 1) * HG_SUBTILE:(j + 2) * HG_SUBTILE, :], mod_ref)
        else:
            pieces = in_projection(0, hn_ref[...], modn_ref)
        for blk in range(blocks_per_sub):
            rows = slice(blk * HG_BLOCK, (blk + 1) * HG_BLOCK)
            out_rows = slice(j * HG_SUBTILE + blk * HG_BLOCK, j * HG_SUBTILE + (blk + 1) * HG_BLOCK)
            _hgrn_block(proj_scrs[j], mix_scr, rows, out_rows, lb, ng_ref, state, masks, pieces[blk])
    y = jnp.dot(mix_scr[...], w_out_ref[...], preferred_element_type=F32)
    _residual_update(y, h_ref, mod_ref, g_out_ref, 2, o_ref)


def _hgrn_layer(h, mod_l, g_in, g_out, lower_bounds, norm_g, w_in, w_out, layer_idx, tm=512):
    bsz, seq, d = h.shape
    tok = lambda b, i: (b, i, 0)
    const = lambda b, i: (0, 0)
    n_tiles = seq // tm

    def next_tile(b, i):
        t = jnp.minimum(b * n_tiles + i + 1, bsz * n_tiles - 1)
        return t // n_tiles, t % n_tiles

    def next_rows(b, i):
        nb, ni = next_tile(b, i)
        return nb, ni * (tm // HG_SUBTILE), 0

    return pl.pallas_call(
        functools.partial(_hgrn_layer_kernel, layer_idx=layer_idx),
        out_shape=jax.ShapeDtypeStruct(h.shape, F32),
        grid=(bsz, n_tiles),
        in_specs=[
            pl.BlockSpec((None, tm, d), tok),
            pl.BlockSpec((None, HG_SUBTILE, d), next_rows),
            pl.BlockSpec((None, 6, d), lambda b, i: (b, 0, 0)),
            pl.BlockSpec((None, 6, d), lambda b, i: (next_tile(b, i)[0], 0, 0)),
            pl.BlockSpec((1, d), const),
            pl.BlockSpec((1, d), const),
            pl.BlockSpec(lower_bounds.shape, const),
            pl.BlockSpec((1, d), const),
            pl.BlockSpec(w_in.shape, const, pipeline_mode=pl.Buffered(1)),
            pl.BlockSpec(w_out.shape, const, pipeline_mode=pl.Buffered(1)),
        ],
        out_specs=pl.BlockSpec((None, tm, d), tok),
        scratch_shapes=[pltpu.VMEM((HG_SUBTILE, 4 * d), F32)] * (tm // HG_SUBTILE)
        + [pltpu.VMEM((HG_SUBTILE, d), BF16)] * (tm // HG_SUBTILE)
        + [pltpu.VMEM((tm, d), BF16), pltpu.VMEM((N_HEADS, HEAD_DIM, HEAD_DIM), F32)],
        compiler_params=_params(("arbitrary", "arbitrary"), VMEM_LIMIT_V7X),
        name="hgrn2_layer",
    )(h, h, mod_l, mod_l, g_in, g_out, lower_bounds, norm_g, w_in, w_out)


def _residue_rows(ref, dilation):
    if dilation == 1:
        return ref[...]
    n = ref.shape[0] // dilation
    return jnp.concatenate([ref[pl.ds(r, n, stride=dilation), :] for r in range(dilation)], axis=0)


def _attn_inproj_kernel(h_ref, mod_ref, g_ref, cos_ref, sin_ref, w_ref, o_ref,
                        u_slab, u_perm, cos_k, sin_k, cos_q, sin_q):
    group = pl.program_id(2)
    n_slabs = D_MODEL // HEAD_DIM
    scale = HEAD_DIM ** -0.5

    def set_tables(cos, sin):
        cos_k[...] = cos
        sin_k[...] = sin
        cos_q[...] = cos * scale
        sin_q[...] = sin * scale

    @pl.when(group == 0)
    def _():
        u = _normed_input(h_ref, mod_ref, g_ref, 0)
        u_perm[...] = u.astype(BF16)
        for c in range(n_slabs):
            u_slab[c] = u[:, c * HEAD_DIM:(c + 1) * HEAD_DIM]
        set_tables(cos_ref[...], sin_ref[...])

    for g, dilation in enumerate(DILATIONS):
        if dilation == 1:
            continue

        @pl.when(group == g)
        def _():
            for c in range(n_slabs):
                u_perm[:, c * HEAD_DIM:(c + 1) * HEAD_DIM] = (
                    _residue_rows(u_slab.at[c], dilation).astype(BF16))
            set_tables(_residue_rows(cos_ref, dilation), _residue_rows(sin_ref, dilation))

    for c0 in range(0, 3 * D_MODEL, MXU_COLS_V7X):
        acc = jnp.dot(u_perm[...], w_ref[:, c0:c0 + MXU_COLS_V7X], preferred_element_type=F32)
        kind = c0 // D_MODEL
        for c1 in range(0, MXU_COLS_V7X, HEAD_DIM):
            t = acc[:, c1:c1 + HEAD_DIM]
            if kind < 2:
                cos, sin = (cos_q, sin_q) if kind == 0 else (cos_k, sin_k)
                t = t * cos[...] + pltpu.roll(t, HEAD_DIM // 2, axis=1) * sin[...]
            o_ref[:, c0 + c1:c0 + c1 + HEAD_DIM] = t.astype(o_ref.dtype)


def _attn_in_projection(h, mod_l, gain, rope, w, tm=1024):
    bsz, seq, d = h.shape
    tok = lambda b, i, j: (b, i, 0)
    return pl.pallas_call(
        _attn_inproj_kernel,
        out_shape=jax.ShapeDtypeStruct((len(DILATIONS), bsz, seq // tm, tm, 3 * d), BF16),
        grid=(bsz, seq // tm, len(DILATIONS)),
        in_specs=[
            pl.BlockSpec((None, tm, d), tok),
            pl.BlockSpec((None, 6, d), lambda b, i, j: (b, 0, 0)),
            pl.BlockSpec((1, d), lambda b, i, j: (0, 0)),
            pl.BlockSpec((None, tm, HEAD_DIM), tok),
            pl.BlockSpec((None, tm, HEAD_DIM), tok),
            pl.BlockSpec((d, 3 * d), lambda b, i, j: (0, j)),
        ],
        out_specs=pl.BlockSpec((None, None, None, tm, 3 * d), lambda b, i, j: (j, b, i, 0, 0)),
        scratch_shapes=[pltpu.VMEM((d // HEAD_DIM, tm, HEAD_DIM), F32), pltpu.VMEM((tm, d), BF16),
                        pltpu.VMEM((tm, HEAD_DIM), F32), pltpu.VMEM((tm, HEAD_DIM), F32),
                        pltpu.VMEM((tm, HEAD_DIM), F32), pltpu.VMEM((tm, HEAD_DIM), F32)],
        compiler_params=_params(("parallel", "parallel", "arbitrary"), VMEM_LIMIT_V7X),
        name="attn_in_projection",
    )(h, mod_l, gain, rope[0], rope[1], w)


def _class_rows(ref, r, blk, sl):
    if len(ref.shape) == 3:
        return ref[r, blk * ATT_BLOCK:(blk + 1) * ATT_BLOCK, sl]
    return jnp.concatenate([ref[t, r, :, sl] for t in range(ref.shape[0])], axis=0)


def _pack_bf16_pair(a, b):
    hi = lax.bitcast_convert_type(a.astype(BF16).astype(F32), jnp.uint32)
    lo = lax.bitcast_convert_type(b.astype(BF16).astype(F32), jnp.uint32)
    return hi | (lo >> 16)


def _unpack_bf16_pair(packed):
    a = lax.bitcast_convert_type(packed & jnp.uint32(0xFFFF0000), F32)
    b = lax.bitcast_convert_type(packed << 16, F32)
    return a, b


def _attn_kernel(q_ref, kp_ref, kc_ref, vp_ref, vc_ref, o_ref, lse_ref, *, dilation, n_blocks):
    n_classes = q_ref.shape[-3]
    first_tile = pl.program_id(1) == 0
    class0 = pl.program_id(2) * n_classes
    qi = lax.broadcasted_iota(jnp.int32, (ATT_BLOCK, 2 * ATT_BLOCK), 0)
    ki = lax.broadcasted_iota(jnp.int32, (ATT_BLOCK, 2 * ATT_BLOCK), 1)
    band = (ki >= qi) & (ki <= qi + ATT_BLOCK)
    band_first = band & (jnp.logical_not(first_tile) | (ki >= ATT_BLOCK))
    lane = lax.broadcasted_iota(jnp.int32, (ATT_BLOCK, HEAD_DIM), 1)
    nt = (((1,), (1,)), ((), ()))
    for r in range(n_classes):
        for blk in range(n_blocks):
            if blk == 0:
                k_prev, v_prev, prev_blk, valid = kp_ref, vp_ref, 0, band_first
            else:
                k_prev, v_prev, prev_blk, valid = kc_ref, vc_ref, blk - 1, band
            if dilation == 1:
                rows = slice(blk * ATT_BLOCK, (blk + 1) * ATT_BLOCK)
            else:
                rows = pl.ds(class0 + r + blk * ATT_BLOCK * dilation, ATT_BLOCK, stride=dilation)
            lse_tile = jnp.zeros((ATT_BLOCK, HEAD_DIM), F32)
            even_head = None
            for hd in range(N_HEADS):
                sl = slice(hd * HEAD_DIM, (hd + 1) * HEAD_DIM)
                kw = jnp.concatenate([_class_rows(k_prev, r, prev_blk, sl),
                                      _class_rows(kc_ref, r, blk, sl)], axis=0)
                vw = jnp.concatenate([_class_rows(v_prev, r, prev_blk, sl),
                                      _class_rows(vc_ref, r, blk, sl)], axis=0)
                s = lax.dot_general(_class_rows(q_ref, r, blk, sl), kw, nt, preferred_element_type=F32)
                s = jnp.where(valid, s, -jnp.inf)
                m = jnp.max(s, axis=-1, keepdims=True)
                p = jnp.exp(s - m)
                z = jnp.sum(p, axis=-1, keepdims=True)
                o = jnp.dot(p.astype(BF16), vw, preferred_element_type=F32) / z
                if hd % 2 == 0:
                    even_head = o
                else:
                    o_ref[hd // 2, rows, :] = _pack_bf16_pair(even_head, o)
                lse_tile = jnp.where(lane == hd, m + jnp.log(z), lse_tile)
            lse_ref[rows, :] = lse_tile


def _dilated_attention(qkv, group, dilation):
    _, bsz, n_tiles, tm, n3 = qkv.shape
    d = D_MODEL
    seq = n_tiles * tm
    n = tm // dilation
    view = qkv.reshape(len(DILATIONS), bsz, n_tiles, dilation, n, n3)
    n_blocks = max(min(4, n // ATT_BLOCK), 1)
    rows = n_blocks * ATT_BLOCK
    tokens = rows * dilation
    n_classes = min(dilation, 8)

    if n >= rows:
        per = n // rows

        def cur_rows(which):
            return pl.BlockSpec((None, None, None, n_classes, rows, d),
                                lambda b, i, c: (group, b, i // per, c, i % per, which))

        def prev_rows(which):
            def index(b, i, c):
                row0 = jnp.maximum(i * rows - ATT_BLOCK, 0)
                return (group, b, row0 // n, c, (row0 % n) // ATT_BLOCK, which)
            return pl.BlockSpec((None, None, None, n_classes, ATT_BLOCK, d), index)
    else:
        span = rows // n

        def cur_rows(which):
            return pl.BlockSpec((None, None, span, n_classes, n, d),
                                lambda b, i, c: (group, b, i, c, 0, which))

        def prev_rows(which):
            return pl.BlockSpec((None, None, span, n_classes, n, d),
                                lambda b, i, c: (group, b, jnp.maximum(i - 1, 0), c, 0, which))

    return pl.pallas_call(
        functools.partial(_attn_kernel, dilation=dilation, n_blocks=n_blocks),
        out_shape=(jax.ShapeDtypeStruct((bsz, N_HEADS // 2, seq, HEAD_DIM), jnp.uint32),
                   jax.ShapeDtypeStruct((bsz, seq, HEAD_DIM), F32)),
        grid=(bsz, seq // tokens, dilation // n_classes),
        in_specs=[cur_rows(0), prev_rows(1), cur_rows(1), prev_rows(2), cur_rows(2)],
        out_specs=(pl.BlockSpec((None, N_HEADS // 2, tokens, HEAD_DIM), lambda b, i, c: (b, 0, i, 0)),
                   pl.BlockSpec((None, tokens, HEAD_DIM), lambda b, i, c: (b, i, 0))),
        compiler_params=_params(("parallel", "parallel", "arbitrary"), VMEM_LIMIT_V7X),
        name="dilated_attention",
    )(view, view, view, view, view)


def _attn_outproj_kernel(o1_ref, o2_ref, o3_ref, l1_ref, l2_ref, l3_ref, w_ref, h_ref, mod_ref,
                         g_ref, o_ref):
    l1, l2, l3 = l1_ref[...], l2_ref[...], l3_ref[...]
    m = jnp.maximum(jnp.maximum(l1, l2), l3)
    e1, e2, e3 = jnp.exp(l1 - m), jnp.exp(l2 - m), jnp.exp(l3 - m)
    z = e1 + e2 + e3
    w1, w2 = e1 / z, e2 / z
    pieces = []
    for pair in range(N_HEADS // 2):
        o1s, o2s, o3s = (_unpack_bf16_pair(ref[pair]) for ref in (o1_ref, o2_ref, o3_ref))
        for hd, o1, o2, o3 in zip((2 * pair, 2 * pair + 1), o1s, o2s, o3s):
            o = o3 + w1[:, hd:hd + 1] * (o1 - o3) + w2[:, hd:hd + 1] * (o2 - o3)
            pieces.append(o.astype(BF16))
    x = jnp.concatenate(pieces, axis=1)
    y = jnp.dot(x, w_ref[...], preferred_element_type=F32)
    _residual_update(y, h_ref, mod_ref, g_ref, 2, o_ref)


def _attn_out_projection(outs, w, h, mod_l, gain, tm=512):
    bsz, seq, d = h.shape
    tok = lambda b, i: (b, i, 0)
    o_spec = pl.BlockSpec((None, N_HEADS // 2, tm, HEAD_DIM), lambda b, i: (b, 0, i, 0))
    l_spec = pl.BlockSpec((None, tm, HEAD_DIM), tok)
    return pl.pallas_call(
        _attn_outproj_kernel,
        out_shape=jax.ShapeDtypeStruct(h.shape, F32),
        grid=(bsz, seq // tm),
        in_specs=[o_spec] * len(outs) + [l_spec] * len(outs) + [
            pl.BlockSpec(w.shape, lambda b, i: (0, 0)),
            pl.BlockSpec((None, tm, d), tok),
            pl.BlockSpec((None, 6, d), lambda b, i: (b, 0, 0)),
            pl.BlockSpec((1, d), lambda b, i: (0, 0)),
        ],
        out_specs=pl.BlockSpec((None, tm, d), tok),
        compiler_params=_params(("parallel", "parallel"), VMEM_LIMIT_V7X),
        name="attn_out_projection",
    )(*[o for o, _ in outs], *[l for _, l in outs], w, h, mod_l, gain)


FFN_CHUNK = 256


def _ffn_kernel(h_ref, mod_ref, g_in_ref, g_out_ref, w_in_ref, w_out_ref, o_ref):
    u = _normed_input(h_ref, mod_ref, g_in_ref, 3).astype(BF16)
    acc = jnp.zeros(h_ref.shape, F32)
    for c0 in range(0, D_FF, FFN_CHUNK):
        gate = jnp.dot(u, w_in_ref[:, c0:c0 + FFN_CHUNK], preferred_element_type=F32)
        up = jnp.dot(u, w_in_ref[:, D_FF + c0:D_FF + c0 + FFN_CHUNK], preferred_element_type=F32)
        act = (_silu(gate) * up).astype(BF16)
        acc = acc + jnp.dot(act, w_out_ref[c0:c0 + FFN_CHUNK, :], preferred_element_type=F32)
    _residual_update(acc, h_ref, mod_ref, g_out_ref, 5, o_ref)


def _ffn(h, mod_l, g_in, g_out, w_in, w_out, tm=512):
    bsz, seq, d = h.shape
    tok = lambda b, i: (b, i, 0)
    const = lambda b, i: (0, 0)
    return pl.pallas_call(
        _ffn_kernel,
        out_shape=jax.ShapeDtypeStruct(h.shape, F32),
        grid=(bsz, seq // tm),
        in_specs=[
            pl.BlockSpec((None, tm, d), tok),
            pl.BlockSpec((None, 6, d), lambda b, i: (b, 0, 0)),
            pl.BlockSpec((1, d), const),
            pl.BlockSpec((1, d), const),
            pl.BlockSpec(w_in.shape, const, pipeline_mode=pl.Buffered(1)),
            pl.BlockSpec(w_out.shape, const, pipeline_mode=pl.Buffered(1)),
        ],
        out_specs=pl.BlockSpec((None, tm, d), tok),
        compiler_params=_params(("parallel", "parallel"), VMEM_LIMIT_V7X),
        name="swiglu_ffn",
    )(h, mod_l, g_in, g_out, w_in, w_out)


def kernel(x, c, positions, ada_w, ada_b, norm_g, hgrn_w_in, hgrn_lower_bounds, hgrn_norm_g,
           hgrn_w_out, attn_w_in, attn_w_out, ffn_w_in, ffn_w_out):
    depth = ada_w.shape[0]
    bsz, seq, d = x.shape
    mod = _modulation(c, ada_w, ada_b).reshape(depth, bsz, 6, d)
    rope = _rope_tables(positions)
    h = x
    for layer in range(depth):
        mod_l = mod[layer]
        gains = norm_g[layer].reshape(4, 1, d)
        idx = layer // 2
        if layer % 2 == 0:
            h = _hgrn_layer(h, mod_l, gains[0], gains[1], hgrn_lower_bounds,
                            hgrn_norm_g[idx].reshape(1, d), hgrn_w_in[idx].astype(BF16),
                            hgrn_w_out[idx].astype(BF16), idx)
        else:
            w_in = _rope_layout(attn_w_in[idx]).astype(BF16)
            qkv = _attn_in_projection(h, mod_l, gains[0], rope, w_in)
            outs = [_dilated_attention(qkv, g, dil) for g, dil in enumerate(DILATIONS)]
            h = _attn_out_projection(outs, attn_w_out[idx].astype(BF16), h, mod_l, gains[1])
        h = _ffn(h, mod_l, gains[2], gains[3], ffn_w_in[layer].astype(BF16),
                 ffn_w_out[layer].astype(BF16))
    return h
```

```python
import functools

import jax
import jax.numpy as jnp
from jax import lax
from jax.experimental import pallas as pl
from jax.experimental.pallas import tpu as pltpu

F32 = jnp.float32
BF16 = jnp.bfloat16

D_MODEL = 1024
HEAD_DIM = 128
N_HEADS = D_MODEL // HEAD_DIM
HG_CHUNK = 32
HG_BLOCK = 128
HG_SUBTILE = 256
HG_PRE_PIECES = 3
ATT_BLOCK = 128
DILATIONS = (1, 4, 16)
ROPE_DIM = HEAD_DIM // 4
ROPE_HALF = ROPE_DIM // 2
ROPE_THETA = 500000.0
D_FF = -(-(8 * D_MODEL) // (3 * 256)) * 256
NORM_EPS = 1e-6
LOG2_E = 1.4426950408889634
MXU_COLS_V7X = 256
VMEM_LIMIT_V7X = 56 * 1024 * 1024


def _params(sem, vmem=None):
    return pltpu.CompilerParams(dimension_semantics=sem, vmem_limit_bytes=vmem)


def _rms(x):
    return x * lax.rsqrt(jnp.mean(x * x, axis=-1, keepdims=True) + NORM_EPS)


def _silu(x):
    return x * jax.nn.sigmoid(x)


def _silu_tanh(x):
    half = 0.5 * x
    return half + half * jnp.tanh(half)


def _normed_input(h_ref, mod_ref, g_ref, shift_row):
    y = _rms(h_ref[...]) * g_ref[...]
    return y * (1.0 + mod_ref[shift_row + 1:shift_row + 2, :]) + mod_ref[shift_row:shift_row + 1, :]


def _residual_update(y, h_ref, mod_ref, g_ref, gate_row, o_ref):
    o_ref[...] = h_ref[...] + (1.0 + mod_ref[gate_row:gate_row + 1, :]) * (_rms(y) * g_ref[...])


def _mod_kernel(c_ref, w_ref, b_ref, o_ref):
    cond = _silu(c_ref[...])
    o_ref[...] = jnp.dot(cond, w_ref[...], preferred_element_type=F32,
                         precision=lax.Precision.HIGHEST) + b_ref[...]


def _modulation(c, ada_w, ada_b):
    depth, d, n = ada_w.shape
    bsz = c.shape[0]
    tn = 1536
    return pl.pallas_call(
        _mod_kernel,
        out_shape=jax.ShapeDtypeStruct((depth, bsz, n), F32),
        grid=(depth, n // tn),
        in_specs=[
            pl.BlockSpec((bsz, d), lambda l, j: (0, 0)),
            pl.BlockSpec((None, d, tn), lambda l, j: (l, 0, j)),
            pl.BlockSpec((None, 1, tn), lambda l, j: (l, 0, j)),
        ],
        out_specs=pl.BlockSpec((None, bsz, tn), lambda l, j: (l, 0, j)),
        compiler_params=_params(("parallel", "parallel")),
        name="adaln_modulation",
    )(c, ada_w, ada_b.reshape(depth, 1, n))


def _rope_kernel(pos_ref, invf_ref, lo_ref, hi_ref, cos_ref, sin_lo_ref, sin_hi_ref):
    ang = pos_ref[...].astype(F32) * invf_ref[...]
    sin = jnp.sin(ang)
    cos_ref[...] = jnp.cos(ang)
    sin_lo_ref[...] = sin * lo_ref[...]
    sin_hi_ref[...] = sin * hi_ref[...]


def _rope_tables(positions):
    bsz, seq = positions.shape
    ts = 1024
    inv_freq = ROPE_THETA ** (-jnp.arange(0, ROPE_DIM, 2, dtype=F32) / ROPE_DIM)
    rest = jnp.zeros((HEAD_DIM - ROPE_DIM,), F32)
    half0, half1 = jnp.zeros((ROPE_HALF,), F32), jnp.ones((ROPE_HALF,), F32)
    invf = jnp.concatenate([inv_freq, inv_freq, rest]).reshape(1, HEAD_DIM)
    lo = jnp.concatenate([-half1, half0, rest]).reshape(1, HEAD_DIM)
    hi = jnp.concatenate([half0, half1, rest]).reshape(1, HEAD_DIM)
    out = jax.ShapeDtypeStruct((bsz, seq, HEAD_DIM), F32)
    lane_row = pl.BlockSpec((1, HEAD_DIM), lambda b, i: (0, 0))
    table = pl.BlockSpec((None, ts, HEAD_DIM), lambda b, i: (b, i, 0))
    return pl.pallas_call(
        _rope_kernel,
        out_shape=(out, out, out),
        grid=(bsz, seq // ts),
        in_specs=[pl.BlockSpec((None, ts, 1), lambda b, i: (b, i, 0)), lane_row, lane_row, lane_row],
        out_specs=(table, table, table),
        compiler_params=_params(("parallel", "parallel")),
        name="rope_tables",
    )(positions.reshape(bsz, seq, 1), invf, lo, hi)


def _scale_chunks(x, vals):
    parts = []
    for j, v in enumerate(vals):
        piece = x[j * HG_CHUNK:(j + 1) * HG_CHUNK, :]
        parts.append(piece if v is None else piece * v)
    return jnp.concatenate(parts, axis=0)


def _hgrn_block(proj_scr, mix_scr, rows, out_rows, lb, ng_ref, state, masks, interleaved):
    d = D_MODEL
    diag_mask, off_mask, tril2 = masks
    for run in interleaved[0]:
        run()
    sig = jax.nn.sigmoid(proj_scr[rows, d:2 * d])
    log_f = jnp.log(lb + (1.0 - lb) * sig) * LOG2_E
    k = (1.0 - lb) * (1.0 - sig)
    q = _silu_tanh(proj_scr[rows, 0:d])
    hi = log_f.astype(BF16)
    lo = (log_f - hi.astype(F32)).astype(BF16)
    b = jnp.dot(tril2, jnp.concatenate([hi, lo], axis=0), preferred_element_type=F32)
    nchunk = HG_BLOCK // HG_CHUNK
    bl = [b[(j + 1) * HG_CHUNK - 1:(j + 1) * HG_CHUNK, :] for j in range(nchunk)]
    b_tot = bl[0] + bl[1] + bl[2] + bl[3]
    q_dec = q * jnp.exp2(b)
    k_inv = k * jnp.exp2(-b)
    q_mid = _scale_chunks(q_dec, [None, jnp.exp2(-bl[1]), None, jnp.exp2(bl[2])]).astype(BF16)
    k_mid = _scale_chunks(k_inv, [jnp.exp2(bl[0] + bl[1]), jnp.exp2(bl[1]), None, None]).astype(BF16)
    q_blk = _scale_chunks(q_dec, [None, jnp.exp2(bl[0]), jnp.exp2(bl[0] + bl[1]),
                                  jnp.exp2(bl[0] + bl[1] + bl[2])]).astype(BF16)
    k_blk = _scale_chunks(k_inv, [jnp.exp2(b_tot), jnp.exp2(b_tot - bl[0]), jnp.exp2(bl[2] + bl[3]),
                                  jnp.exp2(bl[3])]).astype(BF16)
    q_dec = q_dec.astype(BF16)
    k_inv = k_inv.astype(BF16)
    dec = jnp.exp2(b_tot)

    nt = (((1,), (1,)), ((), ()))
    tn = (((0,), (0,)), ((), ()))
    for hd in range(N_HEADS):
        sl = slice(hd * HEAD_DIM, (hd + 1) * HEAD_DIM)
        v = proj_scr[rows, 2 * d + hd * HEAD_DIM:2 * d + (hd + 1) * HEAD_DIM].astype(BF16)
        gate = proj_scr[rows, 3 * d + hd * HEAD_DIM:3 * d + (hd + 1) * HEAD_DIM]
        a_diag = lax.dot_general(q_dec[:, sl], k_inv[:, sl], nt, preferred_element_type=F32)
        a_off = lax.dot_general(q_mid[:, sl], k_mid[:, sl], nt, preferred_element_type=F32)
        a = jnp.where(diag_mask, a_diag, jnp.where(off_mask, a_off, 0.0)).astype(BF16)
        s0 = state[hd]
        o = jnp.dot(jnp.concatenate([a, q_blk[:, sl]], axis=1),
                    jnp.concatenate([v, s0.astype(BF16)], axis=0), preferred_element_type=F32)
        dec_col = jnp.transpose(jnp.broadcast_to(dec[:, sl], (HEAD_DIM, HEAD_DIM)))
        state[hd] = dec_col * s0 + lax.dot_general(k_blk[:, sl], v, tn, preferred_element_type=F32)
        mix_scr[out_rows, sl] = (_rms(o) * ng_ref[:, sl] * _silu_tanh(gate)).astype(BF16)
        for run in interleaved[1 + hd]:
            run()


def _hgrn_layer_kernel(h_ref, hn_ref, mod_ref, modn_ref, g_in_ref, g_out_ref, lb_ref, ng_ref,
                       w_in_ref, w_out_ref, o_ref, *scratch, layer_idx):
    n_sub = h_ref.shape[0] // HG_SUBTILE
    proj_scrs, u_scrs, (mix_scr, state) = scratch[:n_sub], scratch[n_sub:2 * n_sub], scratch[-2:]
    blocks_per_sub = HG_SUBTILE // HG_BLOCK
    piece = MXU_COLS_V7X
    n_pieces = 4 * D_MODEL // piece

    @pl.when(pl.program_id(1) == 0)
    def _():
        state[...] = jnp.zeros_like(state)

    def in_projection(j, x, shift_scale_ref):
        y = _rms(x) * g_in_ref[...]
        u_scrs[j][...] = (y * (1.0 + shift_scale_ref[1:2, :]) + shift_scale_ref[0:1, :]).astype(BF16)

        def make(c):
            def run():
                cols = slice(c * piece, (c + 1) * piece)
                proj_scrs[j][:, cols] = jnp.dot(u_scrs[j][...], w_in_ref[:, cols],
                                                preferred_element_type=F32)
            return run
        per_block = n_pieces // blocks_per_sub
        slots = []
        for blk in range(blocks_per_sub):
            mine = [make(blk * per_block + c) for c in range(per_block)]
            block_slots = [[] for _ in range(N_HEADS + 1)]
            block_slots[0] = mine[:HG_PRE_PIECES]
            rest = mine[HG_PRE_PIECES:]
            for c, run in enumerate(rest):
                block_slots[1 + c * N_HEADS // len(rest)].append(run)
            slots.append(block_slots)
        return slots

    lbr = lb_ref[...]
    e = jnp.exp(lbr - jnp.max(lbr, axis=0, keepdims=True))
    sm = e / jnp.sum(e, axis=0, keepdims=True)
    csum = sm[0:1, :]
    first = csum
    for r in range(1, layer_idx + 1):
        csum = csum + sm[r:r + 1, :]
    lb = csum - first

    ti = lax.broadcasted_iota(jnp.int32, (HG_BLOCK, HG_BLOCK), 0)
    si = lax.broadcasted_iota(jnp.int32, (HG_BLOCK, HG_BLOCK), 1)
    diag_mask = ((ti // HG_CHUNK) == (si // HG_CHUNK)) & (si <= ti)
    off_mask = (si // HG_CHUNK) < (ti // HG_CHUNK)
    tril = jnp.where(diag_mask, 1.0, 0.0).astype(BF16)
    masks = (diag_mask, off_mask, jnp.concatenate([tril, tril], axis=1))

    @pl.when((pl.program_id(0) == 0) & (pl.program_id(1) == 0))
    def _():
        for block_slots in in_projection(0, h_ref[0:HG_SUBTILE, :], mod_ref):
            for slot in block_slots:
                for run in slot:
                    run()

    for j in range(n_sub):
        if j + 1 < n_sub:
            pieces = in_projection(j + 1, h_ref[(j + 1) * HG_SUBTILE:(j + 2) * HG_SUBTILE, :], mod_ref)
        else:
            pieces = in_projection(0, hn_ref[...], modn_ref)
        for blk in range(blocks_per_sub):
            rows = slice(blk * HG_BLOCK, (blk + 1) * HG_BLOCK)
            out_rows = slice(j * HG_SUBTILE + blk * HG_BLOCK, j * HG_SUBTILE + (blk + 1) * HG_BLOCK)
            _hgrn_block(proj_scrs[j], mix_scr, rows, out_rows, lb, ng_ref, state, masks, pieces[blk])
    y = jnp.dot(mix_scr[...], w_out_ref[...], preferred_element_type=F32)
    _residual_update(y, h_ref, mod_ref, g_out_ref, 2, o_ref)


def _hgrn_layer(h, mod_l, g_in, g_out, lower_bounds, norm_g, w_in, w_out, layer_idx, tm=512):
    bsz, seq, d = h.shape
    tok = lambda b, i: (b, i, 0)
    const = lambda b, i: (0, 0)
    n_tiles = seq // tm

    def next_tile(b, i):
        t = jnp.minimum(b * n_tiles + i + 1, bsz * n_tiles - 1)
        return t // n_tiles, t % n_tiles

    def next_rows(b, i):
        nb, ni = next_tile(b, i)
        return nb, ni * (tm // HG_SUBTILE), 0

    return pl.pallas_call(
        functools.partial(_hgrn_layer_kernel, layer_idx=layer_idx),
        out_shape=jax.ShapeDtypeStruct(h.shape, F32),
        grid=(bsz, n_tiles),
        in_specs=[
            pl.BlockSpec((None, tm, d), tok),
            pl.BlockSpec((None, HG_SUBTILE, d), next_rows),
            pl.BlockSpec((None, 6, d), lambda b, i: (b, 0, 0)),
            pl.BlockSpec((None, 6, d), lambda b, i: (next_tile(b, i)[0], 0, 0)),
            pl.BlockSpec((1, d), const),
            pl.BlockSpec((1, d), const),
            pl.BlockSpec(lower_bounds.shape, const),
            pl.BlockSpec((1, d), const),
            pl.BlockSpec(w_in.shape, const, pipeline_mode=pl.Buffered(1)),
            pl.BlockSpec(w_out.shape, const, pipeline_mode=pl.Buffered(1)),
        ],
        out_specs=pl.BlockSpec((None, tm, d), tok),
        scratch_shapes=[pltpu.VMEM((HG_SUBTILE, 4 * d), F32)] * (tm // HG_SUBTILE)
        + [pltpu.VMEM((HG_SUBTILE, d), BF16)] * (tm // HG_SUBTILE)
        + [pltpu.VMEM((tm, d), BF16), pltpu.VMEM((N_HEADS, HEAD_DIM, HEAD_DIM), F32)],
        compiler_params=_params(("arbitrary", "arbitrary"), VMEM_LIMIT_V7X),
        name="hgrn2_layer",
    )(h, h, mod_l, mod_l, g_in, g_out, lower_bounds, norm_g, w_in, w_out)


def _residue_rows(ref, dilation):
    if dilation == 1:
        return ref[...]
    n = ref.shape[0] // dilation
    return jnp.concatenate([ref[pl.ds(r, n, stride=dilation), :] for r in range(dilation)], axis=0)


def _attn_inproj_kernel(h_ref, mod_ref, g_ref, cos_ref, sin_lo_ref, sin_hi_ref, w_ref, o_ref,
                        u_slab, u_perm, tables):
    group = pl.program_id(2)
    n_slabs = D_MODEL // HEAD_DIM
    scale = HEAD_DIM ** -0.5
    rope_refs = (cos_ref, sin_lo_ref, sin_hi_ref)

    def set_tables(dilation):
        for t, ref in enumerate(rope_refs):
            rows = _residue_rows(ref, dilation)
            tables[t] = rows * scale
            tables[3 + t] = rows

    @pl.when(group == 0)
    def _():
        u = _normed_input(h_ref, mod_ref, g_ref, 0)
        u_perm[...] = u.astype(BF16)
        for c in range(n_slabs):
            u_slab[c] = u[:, c * HEAD_DIM:(c + 1) * HEAD_DIM]
        set_tables(1)

    for g, dilation in enumerate(DILATIONS):
        if dilation == 1:
            continue

        @pl.when(group == g)
        def _():
            for c in range(n_slabs):
                u_perm[:, c * HEAD_DIM:(c + 1) * HEAD_DIM] = (
                    _residue_rows(u_slab.at[c], dilation).astype(BF16))
            set_tables(dilation)

    for c0 in range(0, 3 * D_MODEL, MXU_COLS_V7X):
        acc = jnp.dot(u_perm[...], w_ref[:, c0:c0 + MXU_COLS_V7X], preferred_element_type=F32)
        kind = c0 // D_MODEL
        for c1 in range(0, MXU_COLS_V7X, HEAD_DIM):
            t = acc[:, c1:c1 + HEAD_DIM]
            if kind < 2:
                t = (t * tables[3 * kind]
                     + pltpu.roll(t, HEAD_DIM - ROPE_HALF, axis=1) * tables[3 * kind + 1]
                     + pltpu.roll(t, ROPE_HALF, axis=1) * tables[3 * kind + 2])
            o_ref[:, c0 + c1:c0 + c1 + HEAD_DIM] = t.astype(o_ref.dtype)


def _attn_in_projection(h, mod_l, gain, rope, w, tm=1024):
    bsz, seq, d = h.shape
    tok = lambda b, i, j: (b, i, 0)
    return pl.pallas_call(
        _attn_inproj_kernel,
        out_shape=jax.ShapeDtypeStruct((len(DILATIONS), bsz, seq // tm, tm, 3 * d), BF16),
        grid=(bsz, seq // tm, len(DILATIONS)),
        in_specs=[
            pl.BlockSpec((None, tm, d), tok),
            pl.BlockSpec((None, 6, d), lambda b, i, j: (b, 0, 0)),
            pl.BlockSpec((1, d), lambda b, i, j: (0, 0)),
            pl.BlockSpec((None, tm, HEAD_DIM), tok),
            pl.BlockSpec((None, tm, HEAD_DIM), tok),
            pl.BlockSpec((None, tm, HEAD_DIM), tok),
            pl.BlockSpec((d, 3 * d), lambda b, i, j: (0, j)),
        ],
        out_specs=pl.BlockSpec((None, None, None, tm, 3 * d), lambda b, i, j: (j, b, i, 0, 0)),
        scratch_shapes=[pltpu.VMEM((d // HEAD_DIM, tm, HEAD_DIM), F32), pltpu.VMEM((tm, d), BF16),
                        pltpu.VMEM((6, tm, HEAD_DIM), F32)],
        compiler_params=_params(("parallel", "parallel", "arbitrary"), VMEM_LIMIT_V7X),
        name="attn_in_projection",
    )(h, mod_l, gain, *rope, w)


def _class_rows(ref, r, blk, sl):
    if len(ref.shape) == 3:
        return ref[r, blk * ATT_BLOCK:(blk + 1) * ATT_BLOCK, sl]
    return jnp.concatenate([ref[t, r, :, sl] for t in range(ref.shape[0])], axis=0)


def _pack_bf16_pair(a, b):
    hi = lax.bitcast_convert_type(a.astype(BF16).astype(F32), jnp.uint32)
    lo = lax.bitcast_convert_type(b.astype(BF16).astype(F32), jnp.uint32)
    return hi | (lo >> 16)


def _unpack_bf16_pair(packed):
    a = lax.bitcast_convert_type(packed & jnp.uint32(0xFFFF0000), F32)
    b = lax.bitcast_convert_type(packed << 16, F32)
    return a, b


def _attn_kernel(q_ref, kp_ref, kc_ref, vp_ref, vc_ref, o_ref, lse_ref, *, dilation, n_blocks):
    n_classes = q_ref.shape[-3]
    first_tile = pl.program_id(1) == 0
    class0 = pl.program_id(2) * n_classes
    qi = lax.broadcasted_iota(jnp.int32, (ATT_BLOCK, 2 * ATT_BLOCK), 0)
    ki = lax.broadcasted_iota(jnp.int32, (ATT_BLOCK, 2 * ATT_BLOCK), 1)
    band = (ki >= qi) & (ki <= qi + ATT_BLOCK)
    band_first = band & (jnp.logical_not(first_tile) | (ki >= ATT_BLOCK))
    lane = lax.broadcasted_iota(jnp.int32, (ATT_BLOCK, HEAD_DIM), 1)
    nt = (((1,), (1,)), ((), ()))
    for r in range(n_classes):
        for blk in range(n_blocks):
            if blk == 0:
                k_prev, v_prev, prev_blk, valid = kp_ref, vp_ref, 0, band_first
            else:
                k_prev, v_prev, prev_blk, valid = kc_ref, vc_ref, blk - 1, band
            if dilation == 1:
                rows = slice(blk * ATT_BLOCK, (blk + 1) * ATT_BLOCK)
            else:
                rows = pl.ds(class0 + r + blk * ATT_BLOCK * dilation, ATT_BLOCK, stride=dilation)
            lse_tile = jnp.zeros((ATT_BLOCK, HEAD_DIM), F32)
            even_head = None
            for hd in range(N_HEADS):
                sl = slice(hd * HEAD_DIM, (hd + 1) * HEAD_DIM)
                kw = jnp.concatenate([_class_rows(k_prev, r, prev_blk, sl),
                                      _class_rows(kc_ref, r, blk, sl)], axis=0)
                vw = jnp.concatenate([_class_rows(v_prev, r, prev_blk, sl),
                                      _class_rows(vc_ref, r, blk, sl)], axis=0)
                s = lax.dot_general(_class_rows(q_ref, r, blk, sl), kw, nt, preferred_element_type=F32)
                s = jnp.where(valid, s, -jnp.inf)
                m = jnp.max(s, axis=-1, keepdims=True)
                p = jnp.exp(s - m)
                z = jnp.sum(p, axis=-1, keepdims=True)
                o = jnp.dot(p.astype(BF16), vw, preferred_element_type=F32) / z
                if hd % 2 == 0:
                    even_head = o
                else:
                    o_ref[hd // 2, rows, :] = _pack_bf16_pair(even_head, o)
                lse_tile = jnp.where(lane == hd, m + jnp.log(z), lse_tile)
            lse_ref[rows, :] = lse_tile


def _dilated_attention(qkv, group, dilation):
    _, bsz, n_tiles, tm, n3 = qkv.shape
    d = D_MODEL
    seq = n_tiles * tm
    n = tm // dilation
    view = qkv.reshape(len(DILATIONS), bsz, n_tiles, dilation, n, n3)
    n_blocks = max(min(4, n // ATT_BLOCK), 1)
    rows = n_blocks * ATT_BLOCK
    tokens = rows * dilation
    n_classes = min(dilation, 8)

    if n >= rows:
        per = n // rows

        def cur_rows(which):
            return pl.BlockSpec((None, None, None, n_classes, rows, d),
                                lambda b, i, c: (group, b, i // per, c, i % per, which))

        def prev_rows(which):
            def index(b, i, c):
                row0 = jnp.maximum(i * rows - ATT_BLOCK, 0)
                return (group, b, row0 // n, c, (row0 % n) // ATT_BLOCK, which)
            return pl.BlockSpec((None, None, None, n_classes, ATT_BLOCK, d), index)
    else:
        span = rows // n

        def cur_rows(which):
            return pl.BlockSpec((None, None, span, n_classes, n, d),
                                lambda b, i, c: (group, b, i, c, 0, which))

        def prev_rows(which):
            return pl.BlockSpec((None, None, span, n_classes, n, d),
                                lambda b, i, c: (group, b, jnp.maximum(i - 1, 0), c, 0, which))

    return pl.pallas_call(
        functools.partial(_attn_kernel, dilation=dilation, n_blocks=n_blocks),
        out_shape=(jax.ShapeDtypeStruct((bsz, N_HEADS // 2, seq, HEAD_DIM), jnp.uint32),
                   jax.ShapeDtypeStruct((bsz, seq, HEAD_DIM), F32)),
        grid=(bsz, seq // tokens, dilation // n_classes),
        in_specs=[cur_rows(0), prev_rows(1), cur_rows(1), prev_rows(2), cur_rows(2)],
        out_specs=(pl.BlockSpec((None, N_HEADS // 2, tokens, HEAD_DIM), lambda b, i, c: (b, 0, i, 0)),
                   pl.BlockSpec((None, tokens, HEAD_DIM), lambda b, i, c: (b, i, 0))),
        compiler_params=_params(("parallel", "parallel", "arbitrary"), VMEM_LIMIT_V7X),
        name="dilated_attention",
    )(view, view, view, view, view)


def _attn_outproj_kernel(o1_ref, o2_ref, o3_ref, l1_ref, l2_ref, l3_ref, w_ref, h_ref, mod_ref,
                         g_ref, o_ref):
    l1, l2, l3 = l1_ref[...], l2_ref[...], l3_ref[...]
    m = jnp.maximum(jnp.maximum(l1, l2), l3)
    e1, e2, e3 = jnp.exp(l1 - m), jnp.exp(l2 - m), jnp.exp(l3 - m)
    z = e1 + e2 + e3
    w1, w2 = e1 / z, e2 / z
    pieces = []
    for pair in range(N_HEADS // 2):
        o1s, o2s, o3s = (_unpack_bf16_pair(ref[pair]) for ref in (o1_ref, o2_ref, o3_ref))
        for hd, o1, o2, o3 in zip((2 * pair, 2 * pair + 1), o1s, o2s, o3s):
            o = o3 + w1[:, hd:hd + 1] * (o1 - o3) + w2[:, hd:hd + 1] * (o2 - o3)
            pieces.append(o.astype(BF16))
    x = jnp.concatenate(pieces, axis=1)
    y = jnp.dot(x, w_ref[...], preferred_element_type=F32)
    _residual_update(y, h_ref, mod_ref, g_ref, 2, o_ref)


def _attn_out_projection(outs, w, h, mod_l, gain, tm=512):
    bsz, seq, d = h.shape
    tok = lambda b, i: (b, i, 0)
    o_spec = pl.BlockSpec((None, N_HEADS // 2, tm, HEAD_DIM), lambda b, i: (b, 0, i, 0))
    l_spec = pl.BlockSpec((None, tm, HEAD_DIM), tok)
    return pl.pallas_call(
        _attn_outproj_kernel,
        out_shape=jax.ShapeDtypeStruct(h.shape, F32),
        grid=(bsz, seq // tm),
        in_specs=[o_spec] * len(outs) + [l_spec] * len(outs) + [
            pl.BlockSpec(w.shape, lambda b, i: (0, 0)),
            pl.BlockSpec((None, tm, d), tok),
            pl.BlockSpec((None, 6, d), lambda b, i: (b, 0, 0)),
            pl.BlockSpec((1, d), lambda b, i: (0, 0)),
        ],
        out_specs=pl.BlockSpec((None, tm, d), tok),
        compiler_params=_params(("parallel", "parallel"), VMEM_LIMIT_V7X),
        name="attn_out_projection",
    )(*[o for o, _ in outs], *[l for _, l in outs], w, h, mod_l, gain)


FFN_CHUNK = 256


def _ffn_kernel(h_ref, mod_ref, g_in_ref, g_out_ref, w_in_ref, w_out_ref, o_ref):
    u = _normed_input(h_ref, mod_ref, g_in_ref, 3).astype(BF16)
    acc = jnp.zeros(h_ref.shape, F32)
    for c0 in range(0, D_FF, FFN_CHUNK):
        gate = jnp.dot(u, w_in_ref[:, c0:c0 + FFN_CHUNK], preferred_element_type=F32)
        up = jnp.dot(u, w_in_ref[:, D_FF + c0:D_FF + c0 + FFN_CHUNK], preferred_element_type=F32)
        act = (_silu(gate) * up).astype(BF16)
        acc = acc + jnp.dot(act, w_out_ref[c0:c0 + FFN_CHUNK, :], preferred_element_type=F32)
    _residual_update(acc, h_ref, mod_ref, g_out_ref, 5, o_ref)


def _ffn(h, mod_l, g_in, g_out, w_in, w_out, tm=512):
    bsz, seq, d = h.shape
    tok = lambda b, i: (b, i, 0)
    const = lambda b, i: (0, 0)
    return pl.pallas_call(
        _ffn_kernel,
        out_shape=jax.ShapeDtypeStruct(h.shape, F32),
        grid=(bsz, seq // tm),
        in_specs=[
            pl.BlockSpec((None, tm, d), tok),
            pl.BlockSpec((None, 6, d), lambda b, i: (b, 0, 0)),
            pl.BlockSpec((1, d), const),
            pl.BlockSpec((1, d), const),
            pl.BlockSpec(w_in.shape, const, pipeline_mode=pl.Buffered(1)),
            pl.BlockSpec(w_out.shape, const, pipeline_mode=pl.Buffered(1)),
        ],
        out_specs=pl.BlockSpec((None, tm, d), tok),
        compiler_params=_params(("parallel", "parallel"), VMEM_LIMIT_V7X),
        name="swiglu_ffn",
    )(h, mod_l, g_in, g_out, w_in, w_out)


def kernel(x, c, positions, ada_w, ada_b, norm_g, hgrn_w_in, hgrn_lower_bounds, hgrn_norm_g,
           hgrn_w_out, attn_w_in, attn_w_out, ffn_w_in, ffn_w_out):
    depth = ada_w.shape[0]
    bsz, seq, d = x.shape
    mod = _modulation(c, ada_w, ada_b).reshape(depth, bsz, 6, d)
    rope = _rope_tables(positions)
    h = x
    for layer in range(depth):
        mod_l = mod[layer]
        gains = norm_g[layer].reshape(4, 1, d)
        idx = layer // 2
        if layer % 2 == 0:
            h = _hgrn_layer(h, mod_l, gains[0], gains[1], hgrn_lower_bounds,
                            hgrn_norm_g[idx].reshape(1, d), hgrn_w_in[idx].astype(BF16),
                            hgrn_w_out[idx].astype(BF16), idx)
        else:
            w_in = attn_w_in[idx].astype(BF16)
            qkv = _attn_in_projection(h, mod_l, gains[0], rope, w_in)
            outs = [_dilated_attention(qkv, g, dil) for g, dil in enumerate(DILATIONS)]
            h = _attn_out_projection(outs, attn_w_out[idx].astype(BF16), h, mod_l, gains[1])
        h = _ffn(h, mod_l, gains[2], gains[3], ffn_w_in[layer].astype(BF16),
                 ffn_w_out[layer].astype(BF16))
    return h
```
